```python
import jax, jax.numpy as jnp
from jax import lax
import numpy as np

D_MODEL = 4096
BATCH = 8
SEQ = 2048
DEPTH = 2

CHUNK = 64
N_MIXERS = 2
HEAD_DIM = 128
N_HEADS = D_MODEL // HEAD_DIM
A_KV_HEADS = N_HEADS // 4
A_GROUP = N_HEADS // A_KV_HEADS
IDX_HEADS = N_HEADS
IDX_DIM = 128
TOPK_MAX = 256
SPARSE_Q_BLOCK = 32
A_Q = N_HEADS * HEAD_DIM
A_KV = A_KV_HEADS * HEAD_DIM
A_IQ = IDX_HEADS * IDX_DIM
A_IN = A_Q + 2 * A_KV + A_IQ + IDX_DIM + IDX_HEADS
LEFT_CHUNKS = 8
BAND = (LEFT_CHUNKS + 1) * CHUNK
REL_CLIP = 256
B_HID = N_HEADS * HEAD_DIM
B_IN = 3 * B_HID
MEM_TOKENS = 256
MEM_HEADS = 4
MEM_HEAD_DIM = 128
MEM_DIM = MEM_HEADS * MEM_HEAD_DIM
D_FF = 4 * D_MODEL
ROPE_THETA = 10000.0
EPS = 1e-6
N_A_LAYERS = (DEPTH + N_MIXERS - 1) // N_MIXERS
N_B_LAYERS = DEPTH // N_MIXERS

kernel_name = "hybrid_dsa_chunkband_memory_trunk"


def rmsnorm(x, g):
    x32 = x.astype(jnp.float32)
    y = x32 * lax.rsqrt(jnp.mean(x32 * x32, axis=-1, keepdims=True) + EPS)
    return (y * g.astype(jnp.float32)).astype(x.dtype)


def rope(x, pos):
    half = x.shape[-1] // 2
    inv = ROPE_THETA ** (-jnp.arange(half, dtype=jnp.float32) / half)
    ang = pos.astype(jnp.float32)[:, None] * inv[None, :]
    cos = jnp.cos(ang)[:, None, :]
    sin = jnp.sin(ang)[:, None, :]
    x32 = x.astype(jnp.float32)
    x1, x2 = x32[..., :half], x32[..., half:]
    return jnp.concatenate([x1 * cos - x2 * sin, x2 * cos + x1 * sin], axis=-1).astype(x.dtype)


def dsa_sparse_attention(h, w_in, w_out):
    bsz, seq, _ = h.shape
    proj = h @ w_in
    cuts = np.cumsum([A_Q, A_KV, A_KV, A_IQ, IDX_DIM]).tolist()
    q, k, v, qi, ki, wi = jnp.split(proj, cuts, axis=-1)
    pos = jnp.arange(seq)
    q = rope(q.reshape(bsz, seq, N_HEADS, HEAD_DIM), pos)
    k = rope(k.reshape(bsz, seq, A_KV_HEADS, HEAD_DIM), pos)
    v = v.reshape(bsz, seq, A_KV_HEADS, HEAD_DIM)
    qi = rope(qi.reshape(bsz, seq, IDX_HEADS, IDX_DIM), pos)
    ki = rope(ki[:, :, None, :], pos)[:, :, 0, :]
    wi = wi * (IDX_HEADS ** -0.5)
    top_k = min(TOPK_MAX, seq // 4)
    key_chunk = pos // CHUNK
    n_blocks = seq // SPARSE_Q_BLOCK
    gather = jax.vmap(lambda t, i: t[i])

    def block(i):
        t0 = i * SPARSE_Q_BLOCK
        q_b = lax.dynamic_slice_in_dim(q, t0, SPARSE_Q_BLOCK, axis=1)
        qi_b = lax.dynamic_slice_in_dim(qi, t0, SPARSE_Q_BLOCK, axis=1)
        wi_b = lax.dynamic_slice_in_dim(wi, t0, SPARSE_Q_BLOCK, axis=1)
        q_chunk = (t0 + jnp.arange(SPARSE_Q_BLOCK)) // CHUNK
        admissible = key_chunk[None, :] <= q_chunk[:, None]
        dots = jnp.einsum('bthd,bsd->bths', qi_b, ki).astype(jnp.float32) * (IDX_DIM ** -0.5)
        score = jnp.einsum('bth,bths->bts', wi_b.astype(jnp.float32), jax.nn.relu(dots))
        score = jnp.where(admissible[None], score, -jnp.inf)
        _, idx = lax.top_k(score, top_k)
        valid = key_chunk[idx] <= q_chunk[None, :, None]
        k_sel = gather(k, idx)
        v_sel = gather(v, idx)
        q_g = q_b.reshape(bsz, SPARSE_Q_BLOCK, A_KV_HEADS, A_GROUP, HEAD_DIM)
        logits = jnp.einsum('btkgd,btnkd->btkgn', q_g, k_sel).astype(jnp.float32) * (HEAD_DIM ** -0.5)
        logits = jnp.where(valid[:, :, None, None, :], logits, -jnp.inf)
        p = jax.nn.softmax(logits, axis=-1).astype(v.dtype)
        o = jnp.einsum('btkgn,btnkd->btkgd', p, v_sel)
        return o.reshape(bsz, SPARSE_Q_BLOCK, A_Q)

    out = lax.map(block, jnp.arange(n_blocks))
    out = jnp.moveaxis(out, 0, 1).reshape(bsz, seq, A_Q)
    return out @ w_out


def chunk_band_attention(h, w_in, rel_bias, w_out):
    bsz, seq, _ = h.shape
    q, k, v = jnp.split(h @ w_in, 3, axis=-1)
    q = q.reshape(bsz, seq, N_HEADS, HEAD_DIM)
    k = k.reshape(bsz, seq, N_HEADS, HEAD_DIM)
    v = v.reshape(bsz, seq, N_HEADS, HEAD_DIM)
    pad = LEFT_CHUNKS * CHUNK
    k_pad = jnp.pad(k, ((0, 0), (pad, 0), (0, 0), (0, 0)))
    v_pad = jnp.pad(v, ((0, 0), (pad, 0), (0, 0), (0, 0)))
    t_loc = jnp.arange(CHUNK)
    j_loc = jnp.arange(BAND)
    rel = t_loc[:, None] + pad - j_loc[None, :]
    rel_idx = jnp.clip(rel, -REL_CLIP, REL_CLIP) + REL_CLIP
    bias = rel_bias[:, rel_idx].astype(jnp.float32)
    n_chunks = seq // CHUNK

    def chunk(c):
        q_c = lax.dynamic_slice_in_dim(q, c * CHUNK, CHUNK, axis=1)
        k_c = lax.dynamic_slice_in_dim(k_pad, c * CHUNK, BAND, axis=1)
        v_c = lax.dynamic_slice_in_dim(v_pad, c * CHUNK, BAND, axis=1)
        valid = (c * CHUNK - pad + j_loc) >= 0
        logits = jnp.einsum('bthd,bshd->bhts', q_c, k_c).astype(jnp.float32) * (HEAD_DIM ** -0.5)
        logits = jnp.where(valid[None, None, None, :], logits + bias[None], -jnp.inf)
        p = jax.nn.softmax(logits, axis=-1).astype(v.dtype)
        o = jnp.einsum('bhts,bshd->bthd', p, v_c)
        return o.reshape(bsz, CHUNK, B_HID)

    out = lax.map(chunk, jnp.arange(n_chunks))
    out = jnp.moveaxis(out, 0, 1).reshape(bsz, seq, B_HID)
    return out @ w_out


def memory_cross_attention(h, mem_n, w_q, w_k, w_v, w_o):
    bsz, seq, _ = h.shape
    q = (h @ w_q).reshape(bsz, seq, MEM_HEADS, MEM_HEAD_DIM)
    k = (mem_n @ w_k).reshape(bsz, -1, MEM_HEADS, MEM_HEAD_DIM)
    v = (mem_n @ w_v).reshape(bsz, -1, MEM_HEADS, MEM_HEAD_DIM)
    logits = jnp.einsum('bshd,bmhd->bhsm', q, k).astype(jnp.float32) * (MEM_HEAD_DIM ** -0.5)
    p = jax.nn.softmax(logits, axis=-1).astype(v.dtype)
    o = jnp.einsum('bhsm,bmhd->bshd', p, v).reshape(bsz, seq, MEM_DIM)
    return o @ w_o


def squared_relu_mlp(h, w_up, w_down):
    a = jax.nn.relu(h @ w_up)
    return (a * a) @ w_down


def setup_inputs(seed: int = 0) -> dict:
    key = jax.random.key(seed)
    ks = jax.random.split(key, 20)
    f32 = jnp.float32

    def w(k, shape, fan_in):
        return jax.random.normal(k, shape, f32) * (fan_in ** -0.5)

    def gain(k, shape):
        return 1.0 + 0.02 * jax.random.normal(k, shape, f32)

    return {
        "x": jax.random.normal(ks[0], (BATCH, SEQ, D_MODEL), f32),
        "mem": jax.random.normal(ks[1], (BATCH, MEM_TOKENS, D_MODEL), f32),
        "g_mix": gain(ks[2], (DEPTH, D_MODEL)),
        "g_mem_attn": gain(ks[3], (DEPTH, D_MODEL)),
        "g_ffn": gain(ks[4], (DEPTH, D_MODEL)),
        "g_memory": gain(ks[5], (D_MODEL,)),
        "g_final": gain(ks[6], (D_MODEL,)),
        "w_in_a": w(ks[7], (N_A_LAYERS, D_MODEL, A_IN), D_MODEL),
        "w_out_a": w(ks[8], (N_A_LAYERS, A_Q, D_MODEL), A_Q),
        "w_in_b": w(ks[9], (N_B_LAYERS, D_MODEL, B_IN), D_MODEL),
        "rel_bias": 0.1 * jax.random.normal(ks[10], (N_B_LAYERS, N_HEADS, 2 * REL_CLIP + 1), f32),
        "w_out_b": w(ks[11], (N_B_LAYERS, B_HID, D_MODEL), B_HID),
        "w_mq": w(ks[12], (DEPTH, D_MODEL, MEM_DIM), D_MODEL),
        "w_mk": w(ks[13], (DEPTH, D_MODEL, MEM_DIM), D_MODEL),
        "w_mv": w(ks[14], (DEPTH, D_MODEL, MEM_DIM), D_MODEL),
        "w_mo": w(ks[15], (DEPTH, MEM_DIM, D_MODEL), MEM_DIM),
        "w_up": w(ks[16], (DEPTH, D_MODEL, D_FF), D_MODEL),
        "w_down": w(ks[17], (DEPTH, D_FF, D_MODEL), D_FF),
    }


def reference(x, mem, g_mix, g_mem_attn, g_ffn, g_memory, g_final, w_in_a, w_out_a,
              w_in_b, rel_bias, w_out_b, w_mq, w_mk, w_mv, w_mo, w_up, w_down):
    mem_n = rmsnorm(mem, g_memory)
    for i in range(DEPTH):
        h = rmsnorm(x, g_mix[i])
        j = i // N_MIXERS
        if i % N_MIXERS == 0:
            x = x + dsa_sparse_attention(h, w_in_a[j], w_out_a[j])
        else:
            x = x + chunk_band_attention(h, w_in_b[j], rel_bias[j], w_out_b[j])
        x = x + memory_cross_attention(rmsnorm(x, g_mem_attn[i]), mem_n,
                                       w_mq[i], w_mk[i], w_mv[i], w_mo[i])
        x = x + squared_relu_mlp(rmsnorm(x, g_ffn[i]), w_up[i], w_down[i])
    return rmsnorm(x, g_final)
```

```python
import functools

import jax
import jax.numpy as jnp
from jax import lax
from jax.experimental import pallas as pl
from jax.experimental.pallas import tpu as pltpu

CHUNK = 64
HEAD_DIM = 128
KV_GROUP = 4
IDX_DIM = 128
TOPK_MAX = 256
LEFT_CHUNKS = 8
REL_CLIP = 256
MEM_HEADS = 4
MEM_HEAD_DIM = 128
ROPE_THETA = 10000.0
EPS = 1e-6

LANES = 128
V7X_VMEM_BYTES = 64 * 1024 * 1024
VMEM_LIMIT = V7X_VMEM_BYTES - 8 * 1024 * 1024

NEG_BIG = -1e30
INT_MIN = -(2 ** 31)

F32 = jnp.float32
BF16 = jnp.bfloat16
NT_DIMS = (((1,), (1,)), ((), ()))


def _cparams(sem):
    return pltpu.CompilerParams(dimension_semantics=sem, vmem_limit_bytes=VMEM_LIMIT)


def _rms(x, g):
    return x * lax.rsqrt(jnp.mean(x * x, axis=-1, keepdims=True) + EPS) * g


def _rmsnorm_kernel(x_ref, g_ref, o_ref):
    o_ref[...] = _rms(x_ref[...], g_ref[...]).astype(o_ref.dtype)


def rmsnorm(x, g, out_dtype, tm=256):
    m, d = x.shape
    return pl.pallas_call(
        _rmsnorm_kernel,
        grid=(m // tm,),
        in_specs=[pl.BlockSpec((tm, d), lambda i: (i, 0)),
                  pl.BlockSpec((1, d), lambda i: (0, 0))],
        out_specs=pl.BlockSpec((tm, d), lambda i: (i, 0)),
        out_shape=jax.ShapeDtypeStruct((m, d), out_dtype),
        compiler_params=_cparams(("parallel",)),
        name="rmsnorm",
    )(x, g.reshape(1, d))


def _epilogue(acc, kind, extra_refs, o_ref, scale):
    if kind == "cast":
        o_ref[...] = acc.astype(o_ref.dtype)
    elif kind == "scale":
        o_ref[...] = (acc * scale).astype(o_ref.dtype)
    elif kind == "relu2":
        a = jnp.maximum(acc, 0.0)
        o_ref[...] = (a * a).astype(o_ref.dtype)
    elif kind == "residual":
        o_ref[...] = (extra_refs[0][...] + acc).astype(o_ref.dtype)
    elif kind == "rope":
        cos = extra_refs[0][...]
        sin = extra_refs[1][...]
        for c in range(acc.shape[1] // HEAD_DIM):
            xs = acc[:, c * HEAD_DIM:(c + 1) * HEAD_DIM]
            o_ref[:, c * HEAD_DIM:(c + 1) * HEAD_DIM] = (
                xs * cos + pltpu.roll(xs, HEAD_DIM // 2, 1) * sin).astype(o_ref.dtype)
    else:
        raise ValueError(kind)


def _mm_kernel_1k(a_ref, b_ref, *rest, kind, scale):
    *extra, o_ref = rest
    acc = jnp.dot(a_ref[...], b_ref[...], preferred_element_type=F32)
    _epilogue(acc, kind, extra, o_ref, scale)


def _mm_kernel_nk(a_ref, b_ref, *rest, kind, scale, nk):
    *extra, o_ref, acc_ref = rest
    k = pl.program_id(2)
    part = jnp.dot(a_ref[...], b_ref[...], preferred_element_type=F32)

    @pl.when(k == 0)
    def _():
        acc_ref[...] = part

    @pl.when(jnp.logical_and(k > 0, k < nk - 1))
    def _():
        acc_ref[...] += part

    @pl.when(k == nk - 1)
    def _():
        _epilogue(acc_ref[...] + part, kind, extra, o_ref, scale)


def matmul(a, b, *, bm, bn, bk=None, out_dtype, kind="cast", res=None, rope=None, scale=None):
    m, kdim = a.shape
    n = b.shape[1]
    bk = kdim if bk is None else min(bk, kdim)
    bm, bn = min(bm, m), min(bn, n)
    nk = kdim // bk
    assert m % bm == 0 and n % bn == 0 and kdim % bk == 0
    in_specs = [pl.BlockSpec((bm, bk), lambda i, j, k: (i, k)),
                pl.BlockSpec((bk, bn), lambda i, j, k: (k, j))]
    args = [a, b]
    if kind == "residual":
        in_specs.append(pl.BlockSpec((bm, bn), lambda i, j, k: (i, j)))
        args.append(res)
    if kind == "rope":
        cos, sin, seq = rope
        nb = seq // bm
        assert seq % bm == 0
        for t in (cos, sin):
            in_specs.append(pl.BlockSpec((bm, HEAD_DIM), lambda i, j, k: (i % nb, 0)))
            args.append(t)
    out_spec = pl.BlockSpec((bm, bn), lambda i, j, k: (i, j))
    if nk == 1:
        body = functools.partial(_mm_kernel_1k, kind=kind, scale=scale)
        scratch = []
    else:
        assert nk >= 2
        body = functools.partial(_mm_kernel_nk, kind=kind, scale=scale, nk=nk)
        scratch = [pltpu.VMEM((bm, bn), F32)]
    return pl.pallas_call(
        body,
        grid=(m // bm, n // bn, nk),
        in_specs=in_specs,
        out_specs=out_spec,
        out_shape=jax.ShapeDtypeStruct((m, n), out_dtype),
        scratch_shapes=scratch,
        compiler_params=_cparams(("parallel", "parallel", "arbitrary")),
        name="mm_" + kind,
    )(*args)


def _softmax_pv(s, v):
    m = jnp.max(s, axis=-1, keepdims=True)
    p = jnp.exp(s - m)
    l = jnp.sum(p, axis=-1, keepdims=True)
    o = jnp.dot(p.astype(BF16), v, preferred_element_type=F32)
    return o / l


def _indexer_kernel(qi_ref, ki_ref, wi_ref, o_ref, *, n_idx_heads, q0, topk):
    tq, s_adm = o_ref.shape[1], o_ref.shape[2]
    ki = ki_ref[0]
    wi = wi_ref[...]
    score = jnp.zeros((tq, s_adm), F32)
    for h in range(n_idx_heads):
        d = lax.dot_general(qi_ref[:, h * IDX_DIM:(h + 1) * IDX_DIM], ki, NT_DIMS,
                            preferred_element_type=F32)
        score = score + wi[:, h:h + 1] * jnp.maximum(d, 0.0)

    q_chunk = (q0 + lax.broadcasted_iota(jnp.int32, (tq, s_adm), 0)) // CHUNK
    k_chunk = lax.broadcasted_iota(jnp.int32, (tq, s_adm), 1) // CHUNK
    admissible = k_chunk <= q_chunk

    bits = pltpu.bitcast(score, jnp.int32)
    key = jnp.where(bits < 0, bits ^ jnp.int32(0x7FFFFFFF), bits)
    key = jnp.where(admissible, key, jnp.int32(INT_MIN))

    kf = jnp.float32(topk)
    cnt = jnp.sum(jnp.where(key >= 0, 1.0, 0.0), axis=-1, keepdims=True)
    lo0 = jnp.where(cnt >= kf, jnp.int32(0), jnp.int32(INT_MIN))

    def body(i, lo):
        cand = lo + (jnp.int32(1) << (30 - i))
        c = jnp.sum(jnp.where(key >= cand, 1.0, 0.0), axis=-1, keepdims=True)
        return jnp.where(c >= kf, cand, lo)

    thr = lax.fori_loop(0, 31, body, lo0)
    sel = jnp.logical_and(admissible, key >= thr)
    o_ref[0] = jnp.where(sel, 0.0, NEG_BIG).astype(o_ref.dtype)

    n_sel = jnp.sum(jnp.where(sel, 1.0, 0.0), axis=-1, keepdims=True)

    @pl.when(jnp.max(n_sel) > kf)
    def _():
        gt = key > thr
        eq = jnp.logical_and(admissible, key == thr)
        need = kf - jnp.sum(jnp.where(gt, 1.0, 0.0), axis=-1, keepdims=True)
        idx = lax.broadcasted_iota(jnp.int32, (tq, s_adm), 1)
        nbits = max(1, (s_adm - 1).bit_length())

        def tie_body(i, last):
            cand = last + (jnp.int32(1) << (nbits - 1 - i))
            c = jnp.sum(jnp.where(jnp.logical_and(eq, idx <= cand), 1.0, 0.0), axis=-1, keepdims=True)
            return jnp.where(c < need, cand, last)

        last = lax.fori_loop(0, nbits, tie_body, jnp.full((tq, 1), -1, jnp.int32))
        keep = jnp.logical_or(gt, jnp.logical_and(eq, idx <= last + 1))
        o_ref[0] = jnp.where(keep, 0.0, NEG_BIG).astype(o_ref.dtype)


def indexer_mask(qi, ki3, wi, *, bsz, seq, qb, tq, topk):
    n_idx_heads = qi.shape[1] // IDX_DIM
    s_adm = (qb + 1) * tq
    nqb = seq // tq
    return pl.pallas_call(
        functools.partial(_indexer_kernel, n_idx_heads=n_idx_heads, q0=qb * tq, topk=topk),
        grid=(bsz,),
        in_specs=[pl.BlockSpec((tq, qi.shape[1]), lambda b: (b * nqb + qb, 0)),
                  pl.BlockSpec((1, s_adm, IDX_DIM), lambda b: (b, 0, 0)),
                  pl.BlockSpec((tq, LANES), lambda b: (b * nqb + qb, 0))],
        out_specs=pl.BlockSpec((1, tq, s_adm), lambda b: (b, 0, 0)),
        out_shape=jax.ShapeDtypeStruct((bsz, tq, s_adm), F32),
        compiler_params=_cparams(("parallel",)),
        name="dsa_indexer",
    )(qi, ki3, wi)


def _dsa_attn_kernel(q_ref, k_ref, v_ref, m_ref, o_ref):
    k = k_ref[0]
    v = v_ref[0]
    mask = m_ref[0]
    scale = HEAD_DIM ** -0.5
    for j in range(KV_GROUP):
        qj = q_ref[:, j * HEAD_DIM:(j + 1) * HEAD_DIM]
        s = lax.dot_general(qj, k, NT_DIMS, preferred_element_type=F32) * scale + mask
        o_ref[0, :, j * HEAD_DIM:(j + 1) * HEAD_DIM] = _softmax_pv(s, v).astype(o_ref.dtype)


def dsa_attention(q, k3, v3, mask, *, bsz, seq, qb, tq):
    d_q = q.shape[1]
    n_kv = k3.shape[2] // HEAD_DIM
    gw = KV_GROUP * HEAD_DIM
    s_adm = (qb + 1) * tq
    nqb = seq // tq
    return pl.pallas_call(
        _dsa_attn_kernel,
        grid=(bsz, n_kv),
        in_specs=[pl.BlockSpec((tq, gw), lambda b, g: (b * nqb + qb, g)),
                  pl.BlockSpec((1, s_adm, HEAD_DIM), lambda b, g: (b, 0, g)),
                  pl.BlockSpec((1, s_adm, HEAD_DIM), lambda b, g: (b, 0, g)),
                  pl.BlockSpec((1, tq, s_adm), lambda b, g: (b, 0, 0))],
        out_specs=pl.BlockSpec((1, tq, gw), lambda b, g: (b, 0, g)),
        out_shape=jax.ShapeDtypeStruct((bsz, tq, d_q), BF16),
        compiler_params=_cparams(("parallel", "parallel")),
        name="dsa_attn",
    )(q, k3, v3, mask)


def _band_attn_kernel(q_ref, k_ref, v_ref, b_ref, o_ref, *, tq, pad):
    seq = q_ref.shape[1]
    width = b_ref.shape[2]
    scale = HEAD_DIM ** -0.5
    for qb in range(seq // tq):
        q0 = qb * tq
        klo = max(0, q0 - pad)
        nk = q0 + tq - klo
        s = lax.dot_general(q_ref[0, q0:q0 + tq, :], k_ref[0, klo:klo + nk, :], NT_DIMS,
                            preferred_element_type=F32)
        s = s * scale + b_ref[0, :, width - nk:]
        o_ref[0, q0:q0 + tq, :] = _softmax_pv(s, v_ref[0, klo:klo + nk, :]).astype(o_ref.dtype)


def band_attention(q3, k3, v3, bias, *, tq):
    bsz, seq, d = q3.shape
    n_heads = d // HEAD_DIM
    pad = LEFT_CHUNKS * CHUNK
    blk = lambda h, b: (b, 0, h)
    return pl.pallas_call(
        functools.partial(_band_attn_kernel, tq=tq, pad=pad),
        grid=(n_heads, bsz),
        in_specs=[pl.BlockSpec((1, seq, HEAD_DIM), blk),
                  pl.BlockSpec((1, seq, HEAD_DIM), blk),
                  pl.BlockSpec((1, seq, HEAD_DIM), blk),
                  pl.BlockSpec((1, tq, tq + pad), lambda h, b: (h, 0, 0))],
        out_specs=pl.BlockSpec((1, seq, HEAD_DIM), blk),
        out_shape=jax.ShapeDtypeStruct((bsz, seq, d), BF16),
        compiler_params=_cparams(("parallel", "parallel")),
        name="band_attn",
    )(q3, k3, v3, bias)


def band_bias(rel_bias, tq):
    pad = LEFT_CHUNKS * CHUNK
    t = jnp.arange(tq)[:, None]
    j = jnp.arange(tq + pad)[None, :]
    rel_idx = jnp.clip(t + pad - j, -REL_CLIP, REL_CLIP) + REL_CLIP
    tc, jc = t // CHUNK, j // CHUNK
    in_band = jnp.logical_and(jc >= tc, jc <= tc + LEFT_CHUNKS)
    return jnp.where(in_band[None], rel_bias[:, rel_idx].astype(F32), NEG_BIG)


def _mem_block_kernel(x_ref, g1_ref, wq_ref, k_ref, v_ref, wo_ref, g2_ref, x_out_ref, h_out_ref):
    x = x_ref[...]
    h = _rms(x, g1_ref[...]).astype(BF16)
    q = jnp.dot(h, wq_ref[...], preferred_element_type=F32).astype(BF16)
    scale = MEM_HEAD_DIM ** -0.5
    outs = []
    for hd in range(MEM_HEADS):
        sl = slice(hd * MEM_HEAD_DIM, (hd + 1) * MEM_HEAD_DIM)
        s = lax.dot_general(q[:, sl], k_ref[0, :, sl], NT_DIMS, preferred_element_type=F32) * scale
        outs.append(_softmax_pv(s, v_ref[0, :, sl]).astype(BF16))
    o = jnp.concatenate(outs, axis=-1)
    y = x + jnp.dot(o, wo_ref[...], preferred_element_type=F32)
    x_out_ref[...] = y
    h_out_ref[...] = _rms(y, g2_ref[...]).astype(h_out_ref.dtype)


def mem_block(x, g1, wq, km3, vm3, wo, g2, *, seq, tm=256):
    m, d = x.shape
    md = wq.shape[1]
    n_mem = km3.shape[1]
    per_b = seq // tm
    const2 = lambda i: (0, 0)
    return pl.pallas_call(
        _mem_block_kernel,
        grid=(m // tm,),
        in_specs=[pl.BlockSpec((tm, d), lambda i: (i, 0)),
                  pl.BlockSpec((1, d), const2),
                  pl.BlockSpec((d, md), const2),
                  pl.BlockSpec((1, n_mem, md), lambda i: (i // per_b, 0, 0)),
                  pl.BlockSpec((1, n_mem, md), lambda i: (i // per_b, 0, 0)),
                  pl.BlockSpec((md, d), const2),
                  pl.BlockSpec((1, d), const2)],
        out_specs=[pl.BlockSpec((tm, d), lambda i: (i, 0)),
                   pl.BlockSpec((tm, d), lambda i: (i, 0))],
        out_shape=[jax.ShapeDtypeStruct((m, d), F32),
                   jax.ShapeDtypeStruct((m, d), BF16)],
        compiler_params=_cparams(("parallel",)),
        name="mem_block",
    )(x, g1.reshape(1, d), wq, km3, vm3, wo, g2.reshape(1, d))


def _rope_tables(seq):
    half = HEAD_DIM // 2
    inv = ROPE_THETA ** (-jnp.arange(half, dtype=F32) / half)
    ang = jnp.arange(seq, dtype=F32)[:, None] * inv[None, :]
    cos, sin = jnp.cos(ang), jnp.sin(ang)
    return jnp.concatenate([cos, cos], axis=-1), jnp.concatenate([-sin, sin], axis=-1)


def _pad_cols(w, n):
    return jnp.pad(w, ((0, 0), (0, n - w.shape[1])))


def kernel(x, mem, g_mix, g_mem_attn, g_ffn, g_memory, g_final, w_in_a, w_out_a, w_in_b, rel_bias,
           w_out_b, w_mq, w_mk, w_mv, w_mo, w_up, w_down):
    bsz, seq, d = x.shape
    depth = g_mix.shape[0]
    m = bsz * seq
    n_heads = d // HEAD_DIM
    n_kv = n_heads // KV_GROUP
    a_q, a_kv, a_iq = n_heads * HEAD_DIM, n_kv * HEAD_DIM, n_heads * IDX_DIM
    n_idx_heads = n_heads
    topk = min(TOPK_MAX, seq // 4)
    tq = 256
    bm = 1024

    xf = x.reshape(m, d)
    cos, sin = _rope_tables(seq)
    rope = (cos, sin, seq)

    n_mem = mem.shape[1]
    mem_n = rmsnorm(mem.reshape(bsz * n_mem, d), g_memory, BF16)

    for i in range(depth):
        j = i // 2
        h = rmsnorm(xf, g_mix[i], BF16)
        if i % 2 == 0:
            w = w_in_a[j].astype(BF16)
            c0, c1, c2, c3, c4 = a_q, a_q + a_kv, a_q + 2 * a_kv, a_q + 2 * a_kv + a_iq, a_q + 2 * a_kv + a_iq + IDX_DIM
            q = matmul(h, w[:, :c0], bm=bm, bn=1024, out_dtype=BF16, kind="rope", rope=rope)
            k = matmul(h, w[:, c0:c1], bm=bm, bn=1024, out_dtype=BF16, kind="rope", rope=rope)
            v = matmul(h, w[:, c1:c2], bm=bm, bn=1024, out_dtype=BF16)
            qi = matmul(h, w[:, c2:c3], bm=bm, bn=1024, out_dtype=BF16, kind="rope", rope=rope)
            ki = matmul(h, w[:, c3:c4], bm=bm, bn=IDX_DIM, out_dtype=BF16, kind="rope", rope=rope)
            wi = matmul(h, _pad_cols(w[:, c4:], LANES), bm=bm, bn=LANES, out_dtype=F32, kind="scale",
                        scale=(n_idx_heads ** -0.5) * (IDX_DIM ** -0.5))
            k3 = k.reshape(bsz, seq, a_kv)
            v3 = v.reshape(bsz, seq, a_kv)
            ki3 = ki.reshape(bsz, seq, IDX_DIM)
            outs = []
            for qb in range(seq // tq):
                mask = indexer_mask(qi, ki3, wi, bsz=bsz, seq=seq, qb=qb, tq=tq, topk=topk)
                outs.append(dsa_attention(q, k3, v3, mask, bsz=bsz, seq=seq, qb=qb, tq=tq))
            o = jnp.stack(outs, axis=1).reshape(m, a_q)
            xf = matmul(o, w_out_a[j].astype(BF16), bm=bm, bn=1024, bk=2048, out_dtype=F32,
                        kind="residual", res=xf)
        else:
            w = w_in_b[j].astype(BF16)
            qkv = [matmul(h, w[:, t * d:(t + 1) * d], bm=bm, bn=1024, out_dtype=BF16).reshape(bsz, seq, d)
                   for t in range(3)]
            o = band_attention(*qkv, band_bias(rel_bias[j], tq), tq=tq).reshape(m, d)
            xf = matmul(o, w_out_b[j].astype(BF16), bm=bm, bn=1024, bk=2048, out_dtype=F32,
                        kind="residual", res=xf)

        md = w_mq.shape[2]
        km = matmul(mem_n, w_mk[i].astype(BF16), bm=bm, bn=md, out_dtype=BF16).reshape(bsz, n_mem, md)
        vm = matmul(mem_n, w_mv[i].astype(BF16), bm=bm, bn=md, out_dtype=BF16).reshape(bsz, n_mem, md)
        xf, h2 = mem_block(xf, g_mem_attn[i], w_mq[i].astype(BF16), km, vm, w_mo[i].astype(BF16),
                           g_ffn[i], seq=seq)

        u = matmul(h2, w_up[i].astype(BF16), bm=bm, bn=1024, out_dtype=BF16, kind="relu2")
        xf = matmul(u, w_down[i].astype(BF16), bm=bm, bn=1024, bk=2048, out_dtype=F32,
                    kind="residual", res=xf)

    return rmsnorm(xf, g_final, F32).reshape(bsz, seq, d)
```

```python
import functools
import math

import jax
import jax.numpy as jnp
from jax import lax
from jax.experimental import pallas as pl
from jax.experimental.pallas import tpu as pltpu

CHUNK = 64
HEAD_DIM = 128
KV_GROUP = 4
IDX_DIM = 128
TOPK_MAX = 256
LEFT_CHUNKS = 8
REL_CLIP = 256
MEM_HEADS = 4
MEM_HEAD_DIM = 128
ROPE_THETA = 10000.0
EPS = 1e-6

LANES = 128
V7X_VMEM_BYTES = 64 * 1024 * 1024
VMEM_LIMIT = V7X_VMEM_BYTES - 8 * 1024 * 1024

LOG2E = 1.4426950408889634
NEG_BIG = -1e30
INT_MIN = -(2 ** 31)

F32 = jnp.float32
BF16 = jnp.bfloat16
NT_DIMS = (((1,), (1,)), ((), ()))


def _cparams(sem):
    return pltpu.CompilerParams(dimension_semantics=sem, vmem_limit_bytes=VMEM_LIMIT)


def _rms(x, g):
    return x * lax.rsqrt(jnp.mean(x * x, axis=-1, keepdims=True) + EPS) * g


def _rmsnorm_kernel(x_ref, g_ref, o_ref):
    o_ref[...] = _rms(x_ref[...], g_ref[...]).astype(o_ref.dtype)


def rmsnorm(x, g, out_dtype, tm=256):
    m, d = x.shape
    return pl.pallas_call(
        _rmsnorm_kernel,
        grid=(m // tm,),
        in_specs=[pl.BlockSpec((tm, d), lambda i: (i, 0)),
                  pl.BlockSpec((1, d), lambda i: (0, 0))],
        out_specs=pl.BlockSpec((tm, d), lambda i: (i, 0)),
        out_shape=jax.ShapeDtypeStruct((m, d), out_dtype),
        compiler_params=_cparams(("parallel",)),
        name="rmsnorm",
    )(x, g.reshape(1, d))


def _epilogue(acc, kind, extra_refs, o_ref, scale):
    if kind == "cast":
        o_ref[...] = acc.astype(o_ref.dtype)
    elif kind == "scale":
        o_ref[...] = (acc * scale).astype(o_ref.dtype)
    elif kind == "relu2":
        a = jnp.maximum(acc, 0.0)
        o_ref[...] = (a * a).astype(o_ref.dtype)
    elif kind == "residual":
        o_ref[...] = (extra_refs[0][...] + acc).astype(o_ref.dtype)
    elif kind == "rope":
        cos = extra_refs[0][...]
        sin = extra_refs[1][...]
        for c in range(acc.shape[1] // HEAD_DIM):
            xs = acc[:, c * HEAD_DIM:(c + 1) * HEAD_DIM]
            o_ref[:, c * HEAD_DIM:(c + 1) * HEAD_DIM] = (
                xs * cos + pltpu.roll(xs, HEAD_DIM // 2, 1) * sin).astype(o_ref.dtype)
    else:
        raise ValueError(kind)


def _mm_kernel_1k(a_ref, b_ref, *rest, kind, scale):
    *extra, o_ref = rest
    acc = jnp.dot(a_ref[...], b_ref[...], preferred_element_type=F32)
    _epilogue(acc, kind, extra, o_ref, scale)


def _mm_kernel_res_nk(a_ref, b_ref, res_ref, o_ref):
    @pl.when(pl.program_id(2) == 0)
    def _():
        o_ref[...] = res_ref[...]

    o_ref[...] += jnp.dot(a_ref[...], b_ref[...], preferred_element_type=F32)


def matmul(a, b, *, bm, bn, bk=None, out_dtype, kind="cast", res=None, rope=None, scale=None,
           col0=0, n=None):
    m, kdim = a.shape
    n = b.shape[1] if n is None else n
    bk = kdim if bk is None else min(bk, kdim)
    bm, bn = min(bm, m), math.gcd(bn, n, col0)
    nk = kdim // bk
    assert m % bm == 0 and n % bn == 0 and kdim % bk == 0 and col0 % bn == 0
    jb0 = col0 // bn
    in_specs = [pl.BlockSpec((bm, bk), lambda i, j, k: (i, k)),
                pl.BlockSpec((bk, bn), lambda i, j, k: (k, j + jb0))]
    args = [a, b]
    if kind == "residual":
        in_specs.append(pl.BlockSpec((bm, bn), lambda i, j, k: (i, j)))
        args.append(res)
    if kind == "rope":
        cos, sin, seq = rope
        nb = seq // bm
        assert seq % bm == 0
        for t in (cos, sin):
            in_specs.append(pl.BlockSpec((bm, HEAD_DIM), lambda i, j, k: (i % nb, 0)))
            args.append(t)
    out_spec = pl.BlockSpec((bm, bn), lambda i, j, k: (i, j))
    if nk == 1:
        body = functools.partial(_mm_kernel_1k, kind=kind, scale=scale)
    else:
        assert kind == "residual" and out_dtype == F32
        body = _mm_kernel_res_nk
    return pl.pallas_call(
        body,
        grid=(m // bm, n // bn, nk),
        in_specs=in_specs,
        out_specs=out_spec,
        out_shape=jax.ShapeDtypeStruct((m, n), out_dtype),
        compiler_params=_cparams(("parallel", "parallel", "arbitrary")),
        name="mm_" + kind,
    )(*args)


def _softmax_pv(t, v, scale):
    m = jnp.max(t, axis=-1, keepdims=True)
    p = jnp.exp2((t - m) * (scale * LOG2E))
    l = jnp.sum(p, axis=-1, keepdims=True)
    o = jnp.dot(p.astype(BF16), v, preferred_element_type=F32)
    return o / l


def _indexer_kernel(qi_ref, ki_ref, wi_ref, o_ref, *, n_idx_heads, q0, topk):
    tq, s_adm = o_ref.shape[1], o_ref.shape[2]
    ki = ki_ref[0]
    wi = wi_ref[...]
    score = jnp.zeros((tq, s_adm), F32)
    for h in range(n_idx_heads):
        d = lax.dot_general(qi_ref[:, h * IDX_DIM:(h + 1) * IDX_DIM], ki, NT_DIMS,
                            preferred_element_type=F32)
        score = score + wi[:, h:h + 1] * jnp.maximum(d, 0.0)

    q_chunk = (q0 + lax.broadcasted_iota(jnp.int32, (tq, s_adm), 0)) // CHUNK
    k_chunk = lax.broadcasted_iota(jnp.int32, (tq, s_adm), 1) // CHUNK
    admissible = k_chunk <= q_chunk

    bits = pltpu.bitcast(score, jnp.int32)
    key = jnp.where(bits < 0, bits ^ jnp.int32(0x7FFFFFFF), bits)
    key = jnp.where(admissible, key, jnp.int32(INT_MIN))

    kf = jnp.float32(topk)
    cnt = jnp.sum(jnp.where(key >= 0, 1.0, 0.0), axis=-1, keepdims=True)
    lo0 = jnp.where(cnt >= kf, jnp.int32(0), jnp.int32(INT_MIN))

    def body(i, lo):
        cand = lo + (jnp.int32(1) << (30 - i))
        c = jnp.sum(jnp.where(key >= cand, 1.0, 0.0), axis=-1, keepdims=True)
        return jnp.where(c >= kf, cand, lo)

    thr = lax.fori_loop(0, 31, body, lo0)
    sel = jnp.logical_and(admissible, key >= thr)
    o_ref[0] = jnp.where(sel, 0.0, NEG_BIG).astype(o_ref.dtype)

    n_sel = jnp.sum(jnp.where(sel, 1.0, 0.0), axis=-1, keepdims=True)

    @pl.when(jnp.max(n_sel) > kf)
    def _():
        gt = key > thr
        eq = jnp.logical_and(admissible, key == thr)
        need = kf - jnp.sum(jnp.where(gt, 1.0, 0.0), axis=-1, keepdims=True)
        idx = lax.broadcasted_iota(jnp.int32, (tq, s_adm), 1)
        nbits = max(1, (s_adm - 1).bit_length())

        def tie_body(i, last):
            cand = last + (jnp.int32(1) << (nbits - 1 - i))
            c = jnp.sum(jnp.where(jnp.logical_and(eq, idx <= cand), 1.0, 0.0), axis=-1, keepdims=True)
            return jnp.where(c < need, cand, last)

        last = lax.fori_loop(0, nbits, tie_body, jnp.full((tq, 1), -1, jnp.int32))
        keep = jnp.logical_or(gt, jnp.logical_and(eq, idx <= last + 1))
        o_ref[0] = jnp.where(keep, 0.0, NEG_BIG).astype(o_ref.dtype)


def indexer_mask(qi, ki3, wi, *, bsz, seq, qb, tq, topk):
    n_idx_heads = qi.shape[1] // IDX_DIM
    s_adm = (qb + 1) * tq
    nqb = seq // tq
    return pl.pallas_call(
        functools.partial(_indexer_kernel, n_idx_heads=n_idx_heads, q0=qb * tq, topk=topk),
        grid=(bsz,),
        in_specs=[pl.BlockSpec((tq, qi.shape[1]), lambda b: (b * nqb + qb, 0)),
                  pl.BlockSpec((1, s_adm, IDX_DIM), lambda b: (b, 0, 0)),
                  pl.BlockSpec((tq, LANES), lambda b: (b * nqb + qb, 0))],
        out_specs=pl.BlockSpec((1, tq, s_adm), lambda b: (b, 0, 0)),
        out_shape=jax.ShapeDtypeStruct((bsz, tq, s_adm), F32),
        compiler_params=_cparams(("parallel",)),
        name="dsa_indexer",
    )(qi, ki3, wi)


def _dsa_attn_kernel(q_ref, k_ref, v_ref, m_ref, *rest):
    o_ref = rest[-1]
    k = k_ref[0]
    v = v_ref[0]
    mask = m_ref[0]
    for j in range(KV_GROUP):
        qj = q_ref[:, j * HEAD_DIM:(j + 1) * HEAD_DIM]
        t = lax.dot_general(qj, k, NT_DIMS, preferred_element_type=F32) + mask
        o_ref[:, j * HEAD_DIM:(j + 1) * HEAD_DIM] = _softmax_pv(t, v, HEAD_DIM ** -0.5).astype(o_ref.dtype)


def dsa_attention(q, k3, v3, mask, o_prev, *, bsz, seq, qb, tq):
    m, d_q = q.shape
    n_kv = k3.shape[2] // HEAD_DIM
    gw = KV_GROUP * HEAD_DIM
    s_adm = (qb + 1) * tq
    nqb = seq // tq
    in_specs = [pl.BlockSpec((tq, gw), lambda b, g: (b * nqb + qb, g)),
                pl.BlockSpec((1, s_adm, HEAD_DIM), lambda b, g: (b, 0, g)),
                pl.BlockSpec((1, s_adm, HEAD_DIM), lambda b, g: (b, 0, g)),
                pl.BlockSpec((1, tq, s_adm), lambda b, g: (b, 0, 0))]
    args = [q, k3, v3, mask]
    aliases = {}
    if o_prev is not None:
        in_specs.append(pl.BlockSpec(memory_space=pl.ANY))
        args.append(o_prev)
        aliases = {4: 0}
    return pl.pallas_call(
        _dsa_attn_kernel,
        grid=(bsz, n_kv),
        in_specs=in_specs,
        out_specs=pl.BlockSpec((tq, gw), lambda b, g: (b * nqb + qb, g)),
        out_shape=jax.ShapeDtypeStruct((m, d_q), BF16),
        input_output_aliases=aliases,
        compiler_params=_cparams(("parallel", "parallel")),
        name="dsa_attn",
    )(*args)


def _band_attn_kernel(q_ref, k_ref, v_ref, g_ref, o_ref, bias_ref, *, tq, pad):
    seq = q_ref.shape[1]
    width = tq + pad
    scale = HEAD_DIM ** -0.5

    @pl.when(pl.program_id(1) == 0)
    def _():
        lp = g_ref.shape[2]
        rows = jnp.broadcast_to(g_ref[0], (tq, lp))
        toe = pltpu.roll(rows, lp - tq + 1, 1, stride=1, stride_axis=0)[:, :width]
        tc = lax.broadcasted_iota(jnp.int32, (tq, width), 0) // CHUNK
        jc = lax.broadcasted_iota(jnp.int32, (tq, width), 1) // CHUNK
        in_band = jnp.logical_and(jc >= tc, jc <= tc + LEFT_CHUNKS)
        bias_ref[...] = jnp.where(in_band, toe * (1.0 / scale), NEG_BIG)

    for qb in range(seq // tq):
        q0 = qb * tq
        klo = max(0, q0 - pad)
        nk = q0 + tq - klo
        t = lax.dot_general(q_ref[0, q0:q0 + tq, :], k_ref[0, klo:klo + nk, :], NT_DIMS,
                            preferred_element_type=F32) + bias_ref[:, width - nk:]
        o_ref[0, q0:q0 + tq, :] = _softmax_pv(t, v_ref[0, klo:klo + nk, :], scale).astype(o_ref.dtype)


def band_attention(q3, k3, v3, rel_rows, *, tq):
    bsz, seq, d = q3.shape
    n_heads = d // HEAD_DIM
    pad = LEFT_CHUNKS * CHUNK
    blk = lambda h, b: (b, 0, h)
    return pl.pallas_call(
        functools.partial(_band_attn_kernel, tq=tq, pad=pad),
        grid=(n_heads, bsz),
        in_specs=[pl.BlockSpec((1, seq, HEAD_DIM), blk),
                  pl.BlockSpec((1, seq, HEAD_DIM), blk),
                  pl.BlockSpec((1, seq, HEAD_DIM), blk),
                  pl.BlockSpec((1, 1, rel_rows.shape[2]), lambda h, b: (h, 0, 0))],
        out_specs=pl.BlockSpec((1, seq, HEAD_DIM), blk),
        out_shape=jax.ShapeDtypeStruct((bsz, seq, d), BF16),
        scratch_shapes=[pltpu.VMEM((tq, tq + pad), F32)],
        compiler_params=_cparams(("parallel", "arbitrary")),
        name="band_attn",
    )(q3, k3, v3, rel_rows)


def band_rel_rows(rel_bias, tq):
    pad = LEFT_CHUNKS * CHUNK
    lp = -(-(2 * tq + pad - 1) // LANES) * LANES
    rel = tq + pad - 1 - jnp.arange(lp)
    idx = jnp.clip(rel, -REL_CLIP, REL_CLIP) + REL_CLIP
    return rel_bias[:, idx].astype(F32)[:, None, :]


def _mem_block_kernel(x_ref, g1_ref, wq_ref, k_ref, v_ref, wo_ref, g2_ref, x_out_ref, h_out_ref):
    x = x_ref[...]
    h = _rms(x, g1_ref[...]).astype(BF16)
    q = jnp.dot(h, wq_ref[...], preferred_element_type=F32).astype(BF16)
    scale = MEM_HEAD_DIM ** -0.5
    outs = []
    for hd in range(MEM_HEADS):
        sl = slice(hd * MEM_HEAD_DIM, (hd + 1) * MEM_HEAD_DIM)
        t = lax.dot_general(q[:, sl], k_ref[0, :, sl], NT_DIMS, preferred_element_type=F32)
        outs.append(_softmax_pv(t, v_ref[0, :, sl], scale).astype(BF16))
    o = jnp.concatenate(outs, axis=-1)
    y = x + jnp.dot(o, wo_ref[...], preferred_element_type=F32)
    x_out_ref[...] = y
    h_out_ref[...] = _rms(y, g2_ref[...]).astype(h_out_ref.dtype)


def mem_block(x, g1, wq, km3, vm3, wo, g2, *, seq, tm=256):
    m, d = x.shape
    md = wq.shape[1]
    n_mem = km3.shape[1]
    per_b = seq // tm
    const2 = lambda i: (0, 0)
    return pl.pallas_call(
        _mem_block_kernel,
        grid=(m // tm,),
        in_specs=[pl.BlockSpec((tm, d), lambda i: (i, 0)),
                  pl.BlockSpec((1, d), const2),
                  pl.BlockSpec((d, md), const2),
                  pl.BlockSpec((1, n_mem, md), lambda i: (i // per_b, 0, 0)),
                  pl.BlockSpec((1, n_mem, md), lambda i: (i // per_b, 0, 0)),
                  pl.BlockSpec((md, d), const2),
                  pl.BlockSpec((1, d), const2)],
        out_specs=[pl.BlockSpec((tm, d), lambda i: (i, 0)),
                   pl.BlockSpec((tm, d), lambda i: (i, 0))],
        out_shape=[jax.ShapeDtypeStruct((m, d), F32),
                   jax.ShapeDtypeStruct((m, d), BF16)],
        compiler_params=_cparams(("parallel",)),
        name="mem_block",
    )(x, g1.reshape(1, d), wq, km3, vm3, wo, g2.reshape(1, d))


def _rope_tables(seq):
    half = HEAD_DIM // 2
    inv = ROPE_THETA ** (-jnp.arange(half, dtype=F32) / half)
    ang = jnp.arange(seq, dtype=F32)[:, None] * inv[None, :]
    cos, sin = jnp.cos(ang), jnp.sin(ang)
    return jnp.concatenate([cos, cos], axis=-1), jnp.concatenate([-sin, sin], axis=-1)


def _pad_cols(w, n):
    return jnp.pad(w, ((0, 0), (0, n - w.shape[1])))


def kernel(x, mem, g_mix, g_mem_attn, g_ffn, g_memory, g_final, w_in_a, w_out_a, w_in_b, rel_bias,
           w_out_b, w_mq, w_mk, w_mv, w_mo, w_up, w_down):
    bsz, seq, d = x.shape
    depth = g_mix.shape[0]
    m = bsz * seq
    n_heads = d // HEAD_DIM
    n_kv = n_heads // KV_GROUP
    a_q, a_kv, a_iq = n_heads * HEAD_DIM, n_kv * HEAD_DIM, n_heads * IDX_DIM
    n_idx_heads = n_heads
    topk = min(TOPK_MAX, seq // 4)
    tq = 256
    bm = 1024

    xf = x.reshape(m, d)
    cos, sin = _rope_tables(seq)
    rope = (cos, sin, seq)

    n_mem = mem.shape[1]
    mem_n = rmsnorm(mem.reshape(bsz * n_mem, d), g_memory, BF16)

    for i in range(depth):
        j = i // 2
        h = rmsnorm(xf, g_mix[i], BF16)
        if i % 2 == 0:
            w = w_in_a[j].astype(BF16)
            c0, c1, c2, c3, c4 = a_q, a_q + a_kv, a_q + 2 * a_kv, a_q + 2 * a_kv + a_iq, a_q + 2 * a_kv + a_iq + IDX_DIM
            q = matmul(h, w, col0=0, n=a_q, bm=bm, bn=1024, out_dtype=BF16, kind="rope", rope=rope)
            k = matmul(h, w, col0=c0, n=a_kv, bm=bm, bn=1024, out_dtype=BF16, kind="rope", rope=rope)
            v = matmul(h, w, col0=c1, n=a_kv, bm=bm, bn=1024, out_dtype=BF16)
            qi = matmul(h, w, col0=c2, n=a_iq, bm=bm, bn=1024, out_dtype=BF16, kind="rope", rope=rope)
            ki = matmul(h, w, col0=c3, n=IDX_DIM, bm=bm, bn=IDX_DIM, out_dtype=BF16, kind="rope", rope=rope)
            wi = matmul(h, _pad_cols(w[:, c4:], LANES), bm=bm, bn=LANES, out_dtype=F32, kind="scale",
                        scale=(n_idx_heads ** -0.5) * (IDX_DIM ** -0.5))
            k3 = k.reshape(bsz, seq, a_kv)
            v3 = v.reshape(bsz, seq, a_kv)
            ki3 = ki.reshape(bsz, seq, IDX_DIM)
            o = None
            for qb in range(seq // tq):
                mask = indexer_mask(qi, ki3, wi, bsz=bsz, seq=seq, qb=qb, tq=tq, topk=topk)
                o = dsa_attention(q, k3, v3, mask, o, bsz=bsz, seq=seq, qb=qb, tq=tq)
            xf = matmul(o, w_out_a[j].astype(BF16), bm=bm, bn=1024, out_dtype=F32,
                        kind="residual", res=xf)
        else:
            w = w_in_b[j].astype(BF16)
            qkv = [matmul(h, w, col0=t * d, n=d, bm=bm, bn=1024, out_dtype=BF16).reshape(bsz, seq, d)
                   for t in range(3)]
            o = band_attention(*qkv, band_rel_rows(rel_bias[j], tq), tq=tq).reshape(m, d)
            xf = matmul(o, w_out_b[j].astype(BF16), bm=bm, bn=1024, out_dtype=F32,
                        kind="residual", res=xf)

        md = w_mq.shape[2]
        km = matmul(mem_n, w_mk[i].astype(BF16), bm=bm, bn=md, out_dtype=BF16).reshape(bsz, n_mem, md)
        vm = matmul(mem_n, w_mv[i].astype(BF16), bm=bm, bn=md, out_dtype=BF16).reshape(bsz, n_mem, md)
        xf, h2 = mem_block(xf, g_mem_attn[i], w_mq[i].astype(BF16), km, vm, w_mo[i].astype(BF16),
                           g_ffn[i], seq=seq)

        u = matmul(h2, w_up[i].astype(BF16), bm=bm, bn=1024, out_dtype=BF16, kind="relu2")
        xf = matmul(u, w_down[i].astype(BF16), bm=bm, bn=1024, bk=4096, out_dtype=F32,
                    kind="residual", res=xf)

    return rmsnorm(xf, g_final, F32).reshape(bsz, seq, d)
```

```python
import functools
import math

import jax
import jax.numpy as jnp
from jax import lax
from jax.experimental import pallas as pl
from jax.experimental.pallas import tpu as pltpu

CHUNK = 64
HEAD_DIM = 128
KV_GROUP = 4
IDX_DIM = 128
TOPK_MAX = 256
LEFT_CHUNKS = 8
REL_CLIP = 256
MEM_HEADS = 4
MEM_HEAD_DIM = 128
ROPE_THETA = 10000.0
EPS = 1e-6

LANES = 128
V7X_VMEM_BYTES = 64 * 1024 * 1024
VMEM_LIMIT = V7X_VMEM_BYTES - 8 * 1024 * 1024

RES_DOT_CHUNK = 512

LOG2E = 1.4426950408889634
NEG_BIG = -1e30
INT_MIN = -(2 ** 31)

F32 = jnp.float32
BF16 = jnp.bfloat16
NT_DIMS = (((1,), (1,)), ((), ()))


def _cparams(sem):
    return pltpu.CompilerParams(dimension_semantics=sem, vmem_limit_bytes=VMEM_LIMIT)


def _rms(x, g):
    return x * lax.rsqrt(jnp.mean(x * x, axis=-1, keepdims=True) + EPS) * g


def _rmsnorm_kernel(x_ref, g_ref, o_ref):
    o_ref[...] = _rms(x_ref[...], g_ref[...]).astype(o_ref.dtype)


def rmsnorm(x, g, out_dtype, tm=256):
    m, d = x.shape
    return pl.pallas_call(
        _rmsnorm_kernel,
        grid=(m // tm,),
        in_specs=[pl.BlockSpec((tm, d), lambda i: (i, 0)),
                  pl.BlockSpec((1, d), lambda i: (0, 0))],
        out_specs=pl.BlockSpec((tm, d), lambda i: (i, 0)),
        out_shape=jax.ShapeDtypeStruct((m, d), out_dtype),
        compiler_params=_cparams(("parallel",)),
        name="rmsnorm",
    )(x, g.reshape(1, d))


def _epilogue(acc, kind, extra_refs, o_ref, scale):
    if kind == "cast":
        o_ref[...] = acc.astype(o_ref.dtype)
    elif kind == "scale":
        o_ref[...] = (acc * scale).astype(o_ref.dtype)
    elif kind == "relu2":
        a = jnp.maximum(acc, 0.0)
        o_ref[...] = (a * a).astype(o_ref.dtype)
    elif kind == "residual":
        o_ref[...] = (extra_refs[0][...] + acc).astype(o_ref.dtype)
    elif kind == "rope":
        cos = extra_refs[0][...]
        sin = extra_refs[1][...]
        for c in range(acc.shape[1] // HEAD_DIM):
            xs = acc[:, c * HEAD_DIM:(c + 1) * HEAD_DIM]
            o_ref[:, c * HEAD_DIM:(c + 1) * HEAD_DIM] = (
                xs * cos + pltpu.roll(xs, HEAD_DIM // 2, 1) * sin).astype(o_ref.dtype)
    else:
        raise ValueError(kind)


def _mm_kernel(a_ref, b_ref, *rest, kind, scale, nk, n_extra, emit_w, has_prev):
    extra = rest[:n_extra]
    outs = rest[n_extra + (1 if has_prev else 0):]
    o_ref = outs[0]
    w_ref = b_ref
    if emit_w:
        w_ref = outs[1]
        w_ref[...] = b_ref[...].astype(BF16)
    if kind != "residual":
        acc = jnp.dot(a_ref[...], w_ref[...], preferred_element_type=F32)
        _epilogue(acc, kind, extra, o_ref, scale)
        return
    if nk > 1:
        @pl.when(pl.program_id(2) == 0)
        def _():
            o_ref[...] = extra[0][...]

    bn = o_ref.shape[1]
    cw = min(bn, RES_DOT_CHUNK)
    for c in range(bn // cw):
        cols = slice(c * cw, (c + 1) * cw)
        part = jnp.dot(a_ref[...], w_ref[:, cols], preferred_element_type=F32)
        if nk == 1:
            o_ref[:, cols] = extra[0][:, cols] + part
        else:
            o_ref[:, cols] += part


def _mm_call(a, b, y_prev, *, row0, nrows, col0, n, bm, bn, bk, out_dtype, kind, res, rope, scale,
             emit_w, layer=0):
    m, kdim = a.shape
    nk = kdim // bk
    jb0 = col0 // bn
    assert n % bn == 0 and kdim % bk == 0 and col0 % bn == 0
    assert nk == 1 or (kind == "residual" and out_dtype == F32)
    if b.ndim == 3:
        b_spec = pl.BlockSpec((None, bk, bn), lambda i, j, k: (layer, k, j + jb0))
    else:
        b_spec = pl.BlockSpec((bk, bn), lambda i, j, k: (k, j + jb0))
    in_specs = [pl.BlockSpec((bm, bk), lambda i, j, k: (i + row0, k)), b_spec]
    args = [a, b]
    if kind == "residual":
        in_specs.append(pl.BlockSpec((bm, bn), lambda i, j, k: (i + row0, j)))
        args.append(res)
    if kind == "rope":
        cos, sin, seq = rope
        nb = seq // bm
        assert seq % bm == 0
        for t in (cos, sin):
            in_specs.append(pl.BlockSpec((bm, HEAD_DIM), lambda i, j, k: ((i + row0) % nb, 0)))
            args.append(t)
    n_extra = len(args) - 2
    aliases = {}
    if y_prev is not None:
        aliases = {len(args): 0}
        in_specs.append(pl.BlockSpec(memory_space=pl.ANY))
        args.append(y_prev)
    out_specs = [pl.BlockSpec((bm, bn), lambda i, j, k: (i + row0, j))]
    out_shape = [jax.ShapeDtypeStruct((m, n), out_dtype)]
    if emit_w:
        out_specs.append(pl.BlockSpec((bk, bn), lambda i, j, k: (k, j)))
        out_shape.append(jax.ShapeDtypeStruct((kdim, n), BF16))
    return pl.pallas_call(
        functools.partial(_mm_kernel, kind=kind, scale=scale, nk=nk, n_extra=n_extra, emit_w=emit_w,
                          has_prev=y_prev is not None),
        grid=(nrows, n // bn, nk),
        in_specs=in_specs,
        out_specs=out_specs,
        out_shape=out_shape,
        input_output_aliases=aliases,
        compiler_params=_cparams(("parallel", "parallel", "arbitrary")),
        name="mm_" + kind + ("_w" if emit_w else ""),
    )(*args)


def matmul(a, w32, *, bm, bn, bk=None, out_dtype, kind="cast", res=None, rope=None, scale=None,
           col0=0, n=None, layer=0, bn_first=256, bk_first=None):
    m, kdim = a.shape
    n = w32.shape[-1] if n is None else n
    bk = kdim if bk is None else min(bk, kdim)
    bk_first = bk if bk_first is None else min(bk_first, kdim)
    bm = min(bm, m)
    assert m % bm == 0 and w32.shape[-2] == kdim
    common = dict(n=n, bm=bm, out_dtype=out_dtype, kind=kind, res=res, rope=rope, scale=scale)
    y, wb = _mm_call(a, w32, None, row0=0, nrows=1, col0=col0, bn=math.gcd(bn_first, n, col0),
                     bk=bk_first, emit_w=True, layer=layer, **common)
    if m > bm:
        y = _mm_call(a, wb, y, row0=1, nrows=m // bm - 1, col0=0, bn=math.gcd(bn, n), bk=bk,
                     emit_w=False, **common)[0]
    return y


def _softmax_pv(t, v, scale):
    m = jnp.max(t, axis=-1, keepdims=True)
    p = jnp.exp2((t - m) * (scale * LOG2E))
    l = jnp.sum(p, axis=-1, keepdims=True)
    o = jnp.dot(p.astype(BF16), v, preferred_element_type=F32)
    return o / l


def _indexer_kernel(qi_ref, ki_ref, wi_ref, o_ref, *, n_idx_heads, q0, topk):
    tq, s_adm = o_ref.shape[1], o_ref.shape[2]
    ki = ki_ref[0]
    wi = wi_ref[...]
    score = jnp.zeros((tq, s_adm), F32)
    for h in range(n_idx_heads):
        d = lax.dot_general(qi_ref[:, h * IDX_DIM:(h + 1) * IDX_DIM], ki, NT_DIMS,
                            preferred_element_type=F32)
        score = score + wi[:, h:h + 1] * jnp.maximum(d, 0.0)

    q_chunk = (q0 + lax.broadcasted_iota(jnp.int32, (tq, s_adm), 0)) // CHUNK
    k_chunk = lax.broadcasted_iota(jnp.int32, (tq, s_adm), 1) // CHUNK
    admissible = k_chunk <= q_chunk

    bits = pltpu.bitcast(score, jnp.int32)
    key = jnp.where(bits < 0, bits ^ jnp.int32(0x7FFFFFFF), bits)
    key = jnp.where(admissible, key, jnp.int32(INT_MIN))

    kf = jnp.float32(topk)
    cnt = jnp.sum(jnp.where(key >= 0, 1.0, 0.0), axis=-1, keepdims=True)
    lo0 = jnp.where(cnt >= kf, jnp.int32(0), jnp.int32(INT_MIN))

    def body(i, lo):
        cand = lo + (jnp.int32(1) << (30 - i))
        c = jnp.sum(jnp.where(key >= cand, 1.0, 0.0), axis=-1, keepdims=True)
        return jnp.where(c >= kf, cand, lo)

    thr = lax.fori_loop(0, 31, body, lo0)
    sel = jnp.logical_and(admissible, key >= thr)
    o_ref[0] = jnp.where(sel, 0.0, NEG_BIG).astype(o_ref.dtype)

    n_sel = jnp.sum(jnp.where(sel, 1.0, 0.0), axis=-1, keepdims=True)

    @pl.when(jnp.max(n_sel) > kf)
    def _():
        gt = key > thr
        eq = jnp.logical_and(admissible, key == thr)
        need = kf - jnp.sum(jnp.where(gt, 1.0, 0.0), axis=-1, keepdims=True)
        idx = lax.broadcasted_iota(jnp.int32, (tq, s_adm), 1)
        nbits = max(1, (s_adm - 1).bit_length())

        def tie_body(i, last):
            cand = last + (jnp.int32(1) << (nbits - 1 - i))
            c = jnp.sum(jnp.where(jnp.logical_and(eq, idx <= cand), 1.0, 0.0), axis=-1, keepdims=True)
            return jnp.where(c < need, cand, last)

        last = lax.fori_loop(0, nbits, tie_body, jnp.full((tq, 1), -1, jnp.int32))
        keep = jnp.logical_or(gt, jnp.logical_and(eq, idx <= last + 1))
        o_ref[0] = jnp.where(keep, 0.0, NEG_BIG).astype(o_ref.dtype)


def indexer_mask(qi, ki3, wi, *, bsz, seq, qb, tq, topk):
    n_idx_heads = qi.shape[1] // IDX_DIM
    s_adm = (qb + 1) * tq
    nqb = seq // tq
    return pl.pallas_call(
        functools.partial(_indexer_kernel, n_idx_heads=n_idx_heads, q0=qb * tq, topk=topk),
        grid=(bsz,),
        in_specs=[pl.BlockSpec((tq, qi.shape[1]), lambda b: (b * nqb + qb, 0)),
                  pl.BlockSpec((1, s_adm, IDX_DIM), lambda b: (b, 0, 0)),
                  pl.BlockSpec((tq, LANES), lambda b: (b * nqb + qb, 0))],
        out_specs=pl.BlockSpec((1, tq, s_adm), lambda b: (b, 0, 0)),
        out_shape=jax.ShapeDtypeStruct((bsz, tq, s_adm), F32),
        compiler_params=_cparams(("parallel",)),
        name="dsa_indexer",
    )(qi, ki3, wi)


def _dsa_attn_kernel(q_ref, k_ref, v_ref, m_ref, *rest):
    o_ref = rest[-1]
    k = k_ref[0]
    v = v_ref[0]
    mask = m_ref[0]
    for j in range(KV_GROUP):
        qj = q_ref[:, j * HEAD_DIM:(j + 1) * HEAD_DIM]
        t = lax.dot_general(qj, k, NT_DIMS, preferred_element_type=F32) + mask
        o_ref[:, j * HEAD_DIM:(j + 1) * HEAD_DIM] = _softmax_pv(t, v, HEAD_DIM ** -0.5).astype(o_ref.dtype)


def dsa_attention(q, k3, v3, mask, o_prev, *, bsz, seq, qb, tq):
    m, d_q = q.shape
    n_kv = k3.shape[2] // HEAD_DIM
    gw = KV_GROUP * HEAD_DIM
    s_adm = (qb + 1) * tq
    nqb = seq // tq
    in_specs = [pl.BlockSpec((tq, gw), lambda b, g: (b * nqb + qb, g)),
                pl.BlockSpec((1, s_adm, HEAD_DIM), lambda b, g: (b, 0, g)),
                pl.BlockSpec((1, s_adm, HEAD_DIM), lambda b, g: (b, 0, g)),
                pl.BlockSpec((1, tq, s_adm), lambda b, g: (b, 0, 0))]
    args = [q, k3, v3, mask]
    aliases = {}
    if o_prev is not None:
        in_specs.append(pl.BlockSpec(memory_space=pl.ANY))
        args.append(o_prev)
        aliases = {4: 0}
    return pl.pallas_call(
        _dsa_attn_kernel,
        grid=(bsz, n_kv),
        in_specs=in_specs,
        out_specs=pl.BlockSpec((tq, gw), lambda b, g: (b * nqb + qb, g)),
        out_shape=jax.ShapeDtypeStruct((m, d_q), BF16),
        input_output_aliases=aliases,
        compiler_params=_cparams(("parallel", "parallel")),
        name="dsa_attn",
    )(*args)


def _band_attn_kernel(q_ref, k_ref, v_ref, g_ref, o_ref, bias_ref, *, tq, pad):
    seq = q_ref.shape[1]
    width = tq + pad
    scale = HEAD_DIM ** -0.5

    @pl.when(pl.program_id(1) == 0)
    def _():
        lp = g_ref.shape[2]
        rows = jnp.broadcast_to(g_ref[0], (tq, lp))
        toe = pltpu.roll(rows, lp - tq + 1, 1, stride=1, stride_axis=0)[:, :width]
        tc = lax.broadcasted_iota(jnp.int32, (tq, width), 0) // CHUNK
        jc = lax.broadcasted_iota(jnp.int32, (tq, width), 1) // CHUNK
        in_band = jnp.logical_and(jc >= tc, jc <= tc + LEFT_CHUNKS)
        bias_ref[...] = jnp.where(in_band, toe * (1.0 / scale), NEG_BIG)

    for qb in range(seq // tq):
        q0 = qb * tq
        klo = max(0, q0 - pad)
        nk = q0 + tq - klo
        t = lax.dot_general(q_ref[0, q0:q0 + tq, :], k_ref[0, klo:klo + nk, :], NT_DIMS,
                            preferred_element_type=F32) + bias_ref[:, width - nk:]
        o_ref[0, q0:q0 + tq, :] = _softmax_pv(t, v_ref[0, klo:klo + nk, :], scale).astype(o_ref.dtype)


def band_attention(q3, k3, v3, rel_rows, *, tq):
    bsz, seq, d = q3.shape
    n_heads = d // HEAD_DIM
    pad = LEFT_CHUNKS * CHUNK
    blk = lambda h, b: (b, 0, h)
    return pl.pallas_call(
        functools.partial(_band_attn_kernel, tq=tq, pad=pad),
        grid=(n_heads, bsz),
        in_specs=[pl.BlockSpec((1, seq, HEAD_DIM), blk),
                  pl.BlockSpec((1, seq, HEAD_DIM), blk),
                  pl.BlockSpec((1, seq, HEAD_DIM), blk),
                  pl.BlockSpec((1, 1, rel_rows.shape[2]), lambda h, b: (h, 0, 0))],
        out_specs=pl.BlockSpec((1, seq, HEAD_DIM), blk),
        out_shape=jax.ShapeDtypeStruct((bsz, seq, d), BF16),
        scratch_shapes=[pltpu.VMEM((tq, tq + pad), F32)],
        compiler_params=_cparams(("parallel", "arbitrary")),
        name="band_attn",
    )(q3, k3, v3, rel_rows)


def band_rel_rows(rel_bias, tq):
    pad = LEFT_CHUNKS * CHUNK
    lp = -(-(2 * tq + pad - 1) // LANES) * LANES
    rel = tq + pad - 1 - jnp.arange(lp)
    idx = jnp.clip(rel, -REL_CLIP, REL_CLIP) + REL_CLIP
    return rel_bias[:, idx].astype(F32)[:, None, :]


def _mem_block_kernel(x_ref, g1_ref, wq_ref, k_ref, v_ref, wo_ref, g2_ref, x_out_ref, h_out_ref):
    x = x_ref[...]
    h = _rms(x, g1_ref[...]).astype(BF16)
    q = jnp.dot(h, wq_ref[...], preferred_element_type=F32).astype(BF16)
    scale = MEM_HEAD_DIM ** -0.5
    outs = []
    for hd in range(MEM_HEADS):
        sl = slice(hd * MEM_HEAD_DIM, (hd + 1) * MEM_HEAD_DIM)
        t = lax.dot_general(q[:, sl], k_ref[0, :, sl], NT_DIMS, preferred_element_type=F32)
        outs.append(_softmax_pv(t, v_ref[0, :, sl], scale).astype(BF16))
    o = jnp.concatenate(outs, axis=-1)
    y = x + jnp.dot(o, wo_ref[...], preferred_element_type=F32)
    x_out_ref[...] = y
    h_out_ref[...] = _rms(y, g2_ref[...]).astype(h_out_ref.dtype)


def mem_block(x, g1, wq, km3, vm3, wo, g2, *, seq, tm=256):
    m, d = x.shape
    md = wq.shape[1]
    n_mem = km3.shape[1]
    per_b = seq // tm
    const2 = lambda i: (0, 0)
    return pl.pallas_call(
        _mem_block_kernel,
        grid=(m // tm,),
        in_specs=[pl.BlockSpec((tm, d), lambda i: (i, 0)),
                  pl.BlockSpec((1, d), const2),
                  pl.BlockSpec((d, md), const2),
                  pl.BlockSpec((1, n_mem, md), lambda i: (i // per_b, 0, 0)),
                  pl.BlockSpec((1, n_mem, md), lambda i: (i // per_b, 0, 0)),
                  pl.BlockSpec((md, d), const2),
                  pl.BlockSpec((1, d), const2)],
        out_specs=[pl.BlockSpec((tm, d), lambda i: (i, 0)),
                   pl.BlockSpec((tm, d), lambda i: (i, 0))],
        out_shape=[jax.ShapeDtypeStruct((m, d), F32),
                   jax.ShapeDtypeStruct((m, d), BF16)],
        compiler_params=_cparams(("parallel",)),
        name="mem_block",
    )(x, g1.reshape(1, d), wq, km3, vm3, wo, g2.reshape(1, d))


def _rope_tables(seq):
    half = HEAD_DIM // 2
    inv = ROPE_THETA ** (-jnp.arange(half, dtype=F32) / half)
    ang = jnp.arange(seq, dtype=F32)[:, None] * inv[None, :]
    cos, sin = jnp.cos(ang), jnp.sin(ang)
    return jnp.concatenate([cos, cos], axis=-1), jnp.concatenate([-sin, sin], axis=-1)


def _pad_cols(w, n):
    return jnp.pad(w, ((0, 0), (0, n - w.shape[1])))


def kernel(x, mem, g_mix, g_mem_attn, g_ffn, g_memory, g_final, w_in_a, w_out_a, w_in_b, rel_bias,
           w_out_b, w_mq, w_mk, w_mv, w_mo, w_up, w_down):
    bsz, seq, d = x.shape
    depth = g_mix.shape[0]
    m = bsz * seq
    n_heads = d // HEAD_DIM
    n_kv = n_heads // KV_GROUP
    a_q, a_kv, a_iq = n_heads * HEAD_DIM, n_kv * HEAD_DIM, n_heads * IDX_DIM
    n_idx_heads = n_heads
    topk = min(TOPK_MAX, seq // 4)
    tq = 256
    bm = 1024

    xf = x.reshape(m, d)
    cos, sin = _rope_tables(seq)
    rope = (cos, sin, seq)

    n_mem = mem.shape[1]
    mem_n = rmsnorm(mem.reshape(bsz * n_mem, d), g_memory, BF16)

    for i in range(depth):
        j = i // 2
        h = rmsnorm(xf, g_mix[i], BF16)
        if i % 2 == 0:
            w = w_in_a
            c0, c1, c2, c3, c4 = a_q, a_q + a_kv, a_q + 2 * a_kv, a_q + 2 * a_kv + a_iq, a_q + 2 * a_kv + a_iq + IDX_DIM
            mm = functools.partial(matmul, h, w, layer=j, bm=bm, bn=1024)
            q = mm(col0=0, n=a_q, out_dtype=BF16, kind="rope", rope=rope)
            k = mm(col0=c0, n=a_kv, out_dtype=BF16, kind="rope", rope=rope)
            v = mm(col0=c1, n=a_kv, out_dtype=BF16)
            qi = mm(col0=c2, n=a_iq, out_dtype=BF16, kind="rope", rope=rope)
            ki = mm(col0=c3, n=IDX_DIM, out_dtype=BF16, kind="rope", rope=rope)
            wi = matmul(h, _pad_cols(w[j, :, c4:], LANES), bm=bm, bn=LANES, out_dtype=F32, kind="scale",
                        scale=(n_idx_heads ** -0.5) * (IDX_DIM ** -0.5))
            k3 = k.reshape(bsz, seq, a_kv)
            v3 = v.reshape(bsz, seq, a_kv)
            ki3 = ki.reshape(bsz, seq, IDX_DIM)
            o = None
            for qb in range(seq // tq):
                mask = indexer_mask(qi, ki3, wi, bsz=bsz, seq=seq, qb=qb, tq=tq, topk=topk)
                o = dsa_attention(q, k3, v3, mask, o, bsz=bsz, seq=seq, qb=qb, tq=tq)
            xf = matmul(o, w_out_a, layer=j, bm=bm, bn=1024, out_dtype=F32, kind="residual", res=xf)
        else:
            qkv = [matmul(h, w_in_b, layer=j, col0=t * d, n=d, bm=bm, bn=1024,
                          out_dtype=BF16).reshape(bsz, seq, d) for t in range(3)]
            o = band_attention(*qkv, band_rel_rows(rel_bias[j], tq), tq=tq).reshape(m, d)
            xf = matmul(o, w_out_b, layer=j, bm=bm, bn=1024, out_dtype=F32, kind="residual", res=xf)

        md = w_mq.shape[2]
        km = matmul(mem_n, w_mk, layer=i, bm=bm, bn=md, out_dtype=BF16).reshape(bsz, n_mem, md)
        vm = matmul(mem_n, w_mv, layer=i, bm=bm, bn=md, out_dtype=BF16).reshape(bsz, n_mem, md)
        xf, h2 = mem_block(xf, g_mem_attn[i], w_mq[i].astype(BF16), km, vm, w_mo[i].astype(BF16),
                           g_ffn[i], seq=seq)

        u = matmul(h2, w_up, layer=i, bm=bm, bn=1024, out_dtype=BF16, kind="relu2")
        xf = matmul(u, w_down, layer=i, bm=bm, bn=1024, bk=4096, bn_first=512, bk_first=2048,
                    out_dtype=F32, kind="residual", res=xf)

    return rmsnorm(xf, g_final, F32).reshape(bsz, seq, d)
```

```python
import functools
import math

import jax
import jax.numpy as jnp
from jax import lax
from jax.experimental import pallas as pl
from jax.experimental.pallas import tpu as pltpu

CHUNK = 64
HEAD_DIM = 128
KV_GROUP = 4
IDX_DIM = 128
TOPK_MAX = 256
LEFT_CHUNKS = 8
REL_CLIP = 256
MEM_HEADS = 4
MEM_HEAD_DIM = 128
ROPE_THETA = 10000.0
EPS = 1e-6

LANES = 128
V7X_VMEM_BYTES = 64 * 1024 * 1024
VMEM_LIMIT = V7X_VMEM_BYTES - 8 * 1024 * 1024

RES_DOT_CHUNK = 512

LOG2E = 1.4426950408889634
NEG_BIG = -1e30
INT_MIN = -(2 ** 31)

F32 = jnp.float32
BF16 = jnp.bfloat16
NT_DIMS = (((1,), (1,)), ((), ()))


def _cparams(sem):
    return pltpu.CompilerParams(dimension_semantics=sem, vmem_limit_bytes=VMEM_LIMIT)


def _rms(x, g):
    return x * lax.rsqrt(jnp.mean(x * x, axis=-1, keepdims=True) + EPS) * g


def _rmsnorm_kernel(x_ref, g_ref, o_ref):
    o_ref[...] = _rms(x_ref[...], g_ref[...]).astype(o_ref.dtype)


def rmsnorm(x, g, out_dtype, tm=256):
    m, d = x.shape
    return pl.pallas_call(
        _rmsnorm_kernel,
        grid=(m // tm,),
        in_specs=[pl.BlockSpec((tm, d), lambda i: (i, 0)),
                  pl.BlockSpec((1, d), lambda i: (0, 0))],
        out_specs=pl.BlockSpec((tm, d), lambda i: (i, 0)),
        out_shape=jax.ShapeDtypeStruct((m, d), out_dtype),
        compiler_params=_cparams(("parallel",)),
        name="rmsnorm",
    )(x, g.reshape(1, d))


def _epilogue(acc, kind, extra_refs, o_ref, scale):
    if kind == "cast":
        o_ref[...] = acc.astype(o_ref.dtype)
    elif kind == "scale":
        o_ref[...] = (acc * scale).astype(o_ref.dtype)
    elif kind == "relu2":
        a = jnp.maximum(acc, 0.0)
        o_ref[...] = (a * a).astype(o_ref.dtype)
    elif kind == "residual":
        o_ref[...] = (extra_refs[0][...] + acc).astype(o_ref.dtype)
    elif kind == "rope":
        cos = extra_refs[0][...]
        sin = extra_refs[1][...]
        for c in range(acc.shape[1] // HEAD_DIM):
            xs = acc[:, c * HEAD_DIM:(c + 1) * HEAD_DIM]
            o_ref[:, c * HEAD_DIM:(c + 1) * HEAD_DIM] = (
                xs * cos + pltpu.roll(xs, HEAD_DIM // 2, 1) * sin).astype(o_ref.dtype)
    else:
        raise ValueError(kind)


def _mm_kernel(a_ref, b_ref, *rest, kind, scale, nk, n_extra, emit_w, w_t, has_prev):
    extra = rest[:n_extra]
    outs = rest[n_extra + (1 if has_prev else 0):]
    o_ref = outs[0]
    w_ref = b_ref
    if emit_w:
        w_ref = outs[1]
        w32 = b_ref[...].T if w_t else b_ref[...]
        w_ref[...] = w32.astype(BF16)
    if kind != "residual":
        acc = jnp.dot(a_ref[...], w_ref[...], preferred_element_type=F32)
        _epilogue(acc, kind, extra, o_ref, scale)
        return
    if nk > 1:
        @pl.when(pl.program_id(2) == 0)
        def _():
            o_ref[...] = extra[0][...]

    bn = o_ref.shape[1]
    cw = min(bn, RES_DOT_CHUNK)
    for c in range(bn // cw):
        cols = slice(c * cw, (c + 1) * cw)
        part = jnp.dot(a_ref[...], w_ref[:, cols], preferred_element_type=F32)
        if nk == 1:
            o_ref[:, cols] = extra[0][:, cols] + part
        else:
            o_ref[:, cols] += part


def _mm_call(a, b, y_prev, *, row0, nrows, col0, n, bm, bn, bk, out_dtype, kind, res, rope, scale,
             emit_w, layer=0, w_t=False):
    m, kdim = a.shape
    nk = kdim // bk
    jb0 = col0 // bn
    assert n % bn == 0 and kdim % bk == 0 and col0 % bn == 0
    assert nk == 1 or (kind == "residual" and out_dtype == F32)
    if w_t:
        assert emit_w and b.ndim == 3
        b_spec = pl.BlockSpec((None, bn, bk), lambda i, j, k: (layer, j + jb0, k))
    elif b.ndim == 3:
        b_spec = pl.BlockSpec((None, bk, bn), lambda i, j, k: (layer, k, j + jb0))
    else:
        b_spec = pl.BlockSpec((bk, bn), lambda i, j, k: (k, j + jb0))
    in_specs = [pl.BlockSpec((bm, bk), lambda i, j, k: (i + row0, k)), b_spec]
    args = [a, b]
    if kind == "residual":
        in_specs.append(pl.BlockSpec((bm, bn), lambda i, j, k: (i + row0, j)))
        args.append(res)
    if kind == "rope":
        cos, sin, seq = rope
        nb = seq // bm
        assert seq % bm == 0
        for t in (cos, sin):
            in_specs.append(pl.BlockSpec((bm, HEAD_DIM), lambda i, j, k: ((i + row0) % nb, 0)))
            args.append(t)
    n_extra = len(args) - 2
    aliases = {}
    if y_prev is not None:
        aliases = {len(args): 0}
        in_specs.append(pl.BlockSpec(memory_space=pl.ANY))
        args.append(y_prev)
    out_specs = [pl.BlockSpec((bm, bn), lambda i, j, k: (i + row0, j))]
    out_shape = [jax.ShapeDtypeStruct((m, n), out_dtype)]
    if emit_w:
        out_specs.append(pl.BlockSpec((bk, bn), lambda i, j, k: (k, j)))
        out_shape.append(jax.ShapeDtypeStruct((kdim, n), BF16))
    return pl.pallas_call(
        functools.partial(_mm_kernel, kind=kind, scale=scale, nk=nk, n_extra=n_extra, emit_w=emit_w,
                          w_t=w_t, has_prev=y_prev is not None),
        grid=(nrows, n // bn, nk),
        in_specs=in_specs,
        out_specs=out_specs,
        out_shape=out_shape,
        input_output_aliases=aliases,
        compiler_params=_cparams(("parallel", "parallel", "arbitrary")),
        name="mm_" + kind + ("_w" if emit_w else ""),
    )(*args)


def matmul(a, w32, *, bm, bn, bk=None, out_dtype, kind="cast", res=None, rope=None, scale=None,
           col0=0, n=None, layer=0, bn_first=512, bk_first=None, w_t=False):
    m, kdim = a.shape
    n = w32.shape[-2 if w_t else -1] if n is None else n
    bk = kdim if bk is None else min(bk, kdim)
    bk_first = bk if bk_first is None else min(bk_first, kdim)
    bm = min(bm, m)
    assert m % bm == 0 and w32.shape[-1 if w_t else -2] == kdim
    common = dict(n=n, bm=bm, out_dtype=out_dtype, kind=kind, res=res, rope=rope, scale=scale)
    y, wb = _mm_call(a, w32, None, row0=0, nrows=1, col0=col0, bn=math.gcd(bn_first, n, col0),
                     bk=bk_first, emit_w=True, layer=layer, w_t=w_t, **common)
    if m > bm:
        y = _mm_call(a, wb, y, row0=1, nrows=m // bm - 1, col0=0, bn=math.gcd(bn, n), bk=bk,
                     emit_w=False, **common)[0]
    return y


def _softmax_pv(t, v, scale):
    m = jnp.max(t, axis=-1, keepdims=True)
    p = jnp.exp2((t - m) * (scale * LOG2E))
    l = jnp.sum(p, axis=-1, keepdims=True)
    o = jnp.dot(p.astype(BF16), v, preferred_element_type=F32)
    return o / l


def _indexer_kernel(qi_ref, ki_ref, wi_ref, o_ref, *, n_idx_heads, q0, topk):
    tq, s_adm = o_ref.shape[1], o_ref.shape[2]
    ki = ki_ref[0]
    wi = wi_ref[...]
    score = jnp.zeros((tq, s_adm), F32)
    for h in range(n_idx_heads):
        d = lax.dot_general(qi_ref[:, h * IDX_DIM:(h + 1) * IDX_DIM], ki, NT_DIMS,
                            preferred_element_type=F32)
        score = score + wi[:, h:h + 1] * jnp.maximum(d, 0.0)

    q_chunk = (q0 + lax.broadcasted_iota(jnp.int32, (tq, s_adm), 0)) // CHUNK
    k_chunk = lax.broadcasted_iota(jnp.int32, (tq, s_adm), 1) // CHUNK
    admissible = k_chunk <= q_chunk

    bits = pltpu.bitcast(score, jnp.int32)
    key = jnp.where(bits < 0, bits ^ jnp.int32(0x7FFFFFFF), bits)
    key = jnp.where(admissible, key, jnp.int32(INT_MIN))

    kf = jnp.float32(topk)
    cnt = jnp.sum(jnp.where(key >= 0, 1.0, 0.0), axis=-1, keepdims=True)
    lo0 = jnp.where(cnt >= kf, jnp.int32(0), jnp.int32(INT_MIN))

    def body(i, lo):
        cand = lo + (jnp.int32(1) << (30 - i))
        c = jnp.sum(jnp.where(key >= cand, 1.0, 0.0), axis=-1, keepdims=True)
        return jnp.where(c >= kf, cand, lo)

    thr = lax.fori_loop(0, 31, body, lo0)
    sel = jnp.logical_and(admissible, key >= thr)
    o_ref[0] = jnp.where(sel, 0.0, NEG_BIG).astype(o_ref.dtype)

    n_sel = jnp.sum(jnp.where(sel, 1.0, 0.0), axis=-1, keepdims=True)

    @pl.when(jnp.max(n_sel) > kf)
    def _():
        gt = key > thr
        eq = jnp.logical_and(admissible, key == thr)
        need = kf - jnp.sum(jnp.where(gt, 1.0, 0.0), axis=-1, keepdims=True)
        idx = lax.broadcasted_iota(jnp.int32, (tq, s_adm), 1)
        nbits = max(1, (s_adm - 1).bit_length())

        def tie_body(i, last):
            cand = last + (jnp.int32(1) << (nbits - 1 - i))
            c = jnp.sum(jnp.where(jnp.logical_and(eq, idx <= cand), 1.0, 0.0), axis=-1, keepdims=True)
            return jnp.where(c < need, cand, last)

        last = lax.fori_loop(0, nbits, tie_body, jnp.full((tq, 1), -1, jnp.int32))
        keep = jnp.logical_or(gt, jnp.logical_and(eq, idx <= last + 1))
        o_ref[0] = jnp.where(keep, 0.0, NEG_BIG).astype(o_ref.dtype)


def indexer_mask(qi, ki3, wi, *, bsz, seq, qb, tq, topk):
    n_idx_heads = qi.shape[1] // IDX_DIM
    s_adm = (qb + 1) * tq
    nqb = seq // tq
    return pl.pallas_call(
        functools.partial(_indexer_kernel, n_idx_heads=n_idx_heads, q0=qb * tq, topk=topk),
        grid=(bsz,),
        in_specs=[pl.BlockSpec((tq, qi.shape[1]), lambda b: (b * nqb + qb, 0)),
                  pl.BlockSpec((1, s_adm, IDX_DIM), lambda b: (b, 0, 0)),
                  pl.BlockSpec((tq, LANES), lambda b: (b * nqb + qb, 0))],
        out_specs=pl.BlockSpec((1, tq, s_adm), lambda b: (b, 0, 0)),
        out_shape=jax.ShapeDtypeStruct((bsz, tq, s_adm), F32),
        compiler_params=_cparams(("parallel",)),
        name="dsa_indexer",
    )(qi, ki3, wi)


def _dsa_attn_kernel(q_ref, k_ref, v_ref, m_ref, *rest):
    o_ref = rest[-1]
    k = k_ref[0]
    v = v_ref[0]
    mask = m_ref[0]
    for j in range(KV_GROUP):
        qj = q_ref[:, j * HEAD_DIM:(j + 1) * HEAD_DIM]
        t = lax.dot_general(qj, k, NT_DIMS, preferred_element_type=F32) + mask
        o_ref[:, j * HEAD_DIM:(j + 1) * HEAD_DIM] = _softmax_pv(t, v, HEAD_DIM ** -0.5).astype(o_ref.dtype)


def dsa_attention(q, k3, v3, mask, o_prev, *, bsz, seq, qb, tq):
    m, d_q = q.shape
    n_kv = k3.shape[2] // HEAD_DIM
    gw = KV_GROUP * HEAD_DIM
    s_adm = (qb + 1) * tq
    nqb = seq // tq
    in_specs = [pl.BlockSpec((tq, gw), lambda b, g: (b * nqb + qb, g)),
                pl.BlockSpec((1, s_adm, HEAD_DIM), lambda b, g: (b, 0, g)),
                pl.BlockSpec((1, s_adm, HEAD_DIM), lambda b, g: (b, 0, g)),
                pl.BlockSpec((1, tq, s_adm), lambda b, g: (b, 0, 0))]
    args = [q, k3, v3, mask]
    aliases = {}
    if o_prev is not None:
        in_specs.append(pl.BlockSpec(memory_space=pl.ANY))
        args.append(o_prev)
        aliases = {4: 0}
    return pl.pallas_call(
        _dsa_attn_kernel,
        grid=(bsz, n_kv),
        in_specs=in_specs,
        out_specs=pl.BlockSpec((tq, gw), lambda b, g: (b * nqb + qb, g)),
        out_shape=jax.ShapeDtypeStruct((m, d_q), BF16),
        input_output_aliases=aliases,
        compiler_params=_cparams(("parallel", "parallel")),
        name="dsa_attn",
    )(*args)


def _band_attn_kernel(q_ref, k_ref, v_ref, g_ref, o_ref, bias_ref, *, tq, pad):
    seq = q_ref.shape[1]
    width = tq + pad
    scale = HEAD_DIM ** -0.5

    @pl.when(pl.program_id(1) == 0)
    def _():
        lp = g_ref.shape[2]
        rows = jnp.broadcast_to(g_ref[0], (tq, lp))
        toe = pltpu.roll(rows, lp - tq + 1, 1, stride=1, stride_axis=0)[:, :width]
        tc = lax.broadcasted_iota(jnp.int32, (tq, width), 0) // CHUNK
        jc = lax.broadcasted_iota(jnp.int32, (tq, width), 1) // CHUNK
        in_band = jnp.logical_and(jc >= tc, jc <= tc + LEFT_CHUNKS)
        bias_ref[...] = jnp.where(in_band, toe * (1.0 / scale), NEG_BIG)

    for qb in range(seq // tq):
        q0 = qb * tq
        klo = max(0, q0 - pad)
        nk = q0 + tq - klo
        t = lax.dot_general(q_ref[0, q0:q0 + tq, :], k_ref[0, klo:klo + nk, :], NT_DIMS,
                            preferred_element_type=F32) + bias_ref[:, width - nk:]
        o_ref[0, q0:q0 + tq, :] = _softmax_pv(t, v_ref[0, klo:klo + nk, :], scale).astype(o_ref.dtype)


def band_attention(q3, k3, v3, rel_rows, *, tq):
    bsz, seq, d = q3.shape
    n_heads = d // HEAD_DIM
    pad = LEFT_CHUNKS * CHUNK
    blk = lambda h, b: (b, 0, h)
    return pl.pallas_call(
        functools.partial(_band_attn_kernel, tq=tq, pad=pad),
        grid=(n_heads, bsz),
        in_specs=[pl.BlockSpec((1, seq, HEAD_DIM), blk),
                  pl.BlockSpec((1, seq, HEAD_DIM), blk),
                  pl.BlockSpec((1, seq, HEAD_DIM), blk),
                  pl.BlockSpec((1, 1, rel_rows.shape[2]), lambda h, b: (h, 0, 0))],
        out_specs=pl.BlockSpec((1, seq, HEAD_DIM), blk),
        out_shape=jax.ShapeDtypeStruct((bsz, seq, d), BF16),
        scratch_shapes=[pltpu.VMEM((tq, tq + pad), F32)],
        compiler_params=_cparams(("parallel", "arbitrary")),
        name="band_attn",
    )(q3, k3, v3, rel_rows)


def band_rel_rows(rel_bias, tq):
    pad = LEFT_CHUNKS * CHUNK
    lp = -(-(2 * tq + pad - 1) // LANES) * LANES
    rel = tq + pad - 1 - jnp.arange(lp)
    idx = jnp.clip(rel, -REL_CLIP, REL_CLIP) + REL_CLIP
    return rel_bias[:, idx].astype(F32)[:, None, :]


def _mem_block_kernel(x_ref, g1_ref, wq_ref, k_ref, v_ref, wo_ref, g2_ref, x_out_ref, h_out_ref):
    x = x_ref[...]
    h = _rms(x, g1_ref[...]).astype(BF16)
    q = jnp.dot(h, wq_ref[...], preferred_element_type=F32).astype(BF16)
    scale = MEM_HEAD_DIM ** -0.5
    outs = []
    for hd in range(MEM_HEADS):
        sl = slice(hd * MEM_HEAD_DIM, (hd + 1) * MEM_HEAD_DIM)
        t = lax.dot_general(q[:, sl], k_ref[0, :, sl], NT_DIMS, preferred_element_type=F32)
        outs.append(_softmax_pv(t, v_ref[0, :, sl], scale).astype(BF16))
    o = jnp.concatenate(outs, axis=-1)
    y = x + jnp.dot(o, wo_ref[...], preferred_element_type=F32)
    x_out_ref[...] = y
    h_out_ref[...] = _rms(y, g2_ref[...]).astype(h_out_ref.dtype)


def mem_block(x, g1, wq, km3, vm3, wo, g2, *, seq, tm=256):
    m, d = x.shape
    md = wq.shape[1]
    n_mem = km3.shape[1]
    per_b = seq // tm
    const2 = lambda i: (0, 0)
    return pl.pallas_call(
        _mem_block_kernel,
        grid=(m // tm,),
        in_specs=[pl.BlockSpec((tm, d), lambda i: (i, 0)),
                  pl.BlockSpec((1, d), const2),
                  pl.BlockSpec((d, md), const2),
                  pl.BlockSpec((1, n_mem, md), lambda i: (i // per_b, 0, 0)),
                  pl.BlockSpec((1, n_mem, md), lambda i: (i // per_b, 0, 0)),
                  pl.BlockSpec((md, d), const2),
                  pl.BlockSpec((1, d), const2)],
        out_specs=[pl.BlockSpec((tm, d), lambda i: (i, 0)),
                   pl.BlockSpec((tm, d), lambda i: (i, 0))],
        out_shape=[jax.ShapeDtypeStruct((m, d), F32),
                   jax.ShapeDtypeStruct((m, d), BF16)],
        compiler_params=_cparams(("parallel",)),
        name="mem_block",
    )(x, g1.reshape(1, d), wq, km3, vm3, wo, g2.reshape(1, d))


def _rope_tables(seq):
    half = HEAD_DIM // 2
    inv = ROPE_THETA ** (-jnp.arange(half, dtype=F32) / half)
    ang = jnp.arange(seq, dtype=F32)[:, None] * inv[None, :]
    cos, sin = jnp.cos(ang), jnp.sin(ang)
    return jnp.concatenate([cos, cos], axis=-1), jnp.concatenate([-sin, sin], axis=-1)


def _pad_cols(w, n):
    return jnp.pad(w, ((0, 0), (0, n - w.shape[1])))


def kernel(x, mem, g_mix, g_mem_attn, g_ffn, g_memory, g_final, w_in_a, w_out_a, w_in_b, rel_bias,
           w_out_b, w_mq, w_mk, w_mv, w_mo, w_up, w_down):
    bsz, seq, d = x.shape
    depth = g_mix.shape[0]
    m = bsz * seq
    n_heads = d // HEAD_DIM
    n_kv = n_heads // KV_GROUP
    a_q, a_kv, a_iq = n_heads * HEAD_DIM, n_kv * HEAD_DIM, n_heads * IDX_DIM
    n_idx_heads = n_heads
    topk = min(TOPK_MAX, seq // 4)
    tq = 256
    bm = 1024

    xf = x.reshape(m, d)
    cos, sin = _rope_tables(seq)
    rope = (cos, sin, seq)

    n_mem = mem.shape[1]
    mem_n = rmsnorm(mem.reshape(bsz * n_mem, d), g_memory, BF16)

    for i in range(depth):
        j = i // 2
        h = rmsnorm(xf, g_mix[i], BF16)
        if i % 2 == 0:
            w = w_in_a
            c0, c1, c2, c3, c4 = a_q, a_q + a_kv, a_q + 2 * a_kv, a_q + 2 * a_kv + a_iq, a_q + 2 * a_kv + a_iq + IDX_DIM
            mm = functools.partial(matmul, h, jnp.swapaxes(w, 1, 2), w_t=True, layer=j, bm=bm, bn=1024)
            q = mm(col0=0, n=a_q, out_dtype=BF16, kind="rope", rope=rope)
            k = mm(col0=c0, n=a_kv, out_dtype=BF16, kind="rope", rope=rope)
            v = mm(col0=c1, n=a_kv, out_dtype=BF16)
            qi = mm(col0=c2, n=a_iq, out_dtype=BF16, kind="rope", rope=rope)
            ki = mm(col0=c3, n=IDX_DIM, out_dtype=BF16, kind="rope", rope=rope)
            wi = matmul(h, _pad_cols(w[j, :, c4:], LANES), bm=bm, bn=LANES, out_dtype=F32, kind="scale",
                        scale=(n_idx_heads ** -0.5) * (IDX_DIM ** -0.5))
            k3 = k.reshape(bsz, seq, a_kv)
            v3 = v.reshape(bsz, seq, a_kv)
            ki3 = ki.reshape(bsz, seq, IDX_DIM)
            o = None
            for qb in range(seq // tq):
                mask = indexer_mask(qi, ki3, wi, bsz=bsz, seq=seq, qb=qb, tq=tq, topk=topk)
                o = dsa_attention(q, k3, v3, mask, o, bsz=bsz, seq=seq, qb=qb, tq=tq)
            xf = matmul(o, w_out_a, layer=j, bm=bm, bn=1024, out_dtype=F32, kind="residual", res=xf)
        else:
            qkv = [matmul(h, w_in_b, layer=j, col0=t * d, n=d, bm=bm, bn=1024,
                          out_dtype=BF16).reshape(bsz, seq, d) for t in range(3)]
            o = band_attention(*qkv, band_rel_rows(rel_bias[j], tq), tq=tq).reshape(m, d)
            xf = matmul(o, w_out_b, layer=j, bm=bm, bn=1024, out_dtype=F32, kind="residual", res=xf)

        md = w_mq.shape[2]
        km = matmul(mem_n, w_mk, layer=i, bm=bm, bn=md, out_dtype=BF16).reshape(bsz, n_mem, md)
        vm = matmul(mem_n, w_mv, layer=i, bm=bm, bn=md, out_dtype=BF16).reshape(bsz, n_mem, md)
        xf, h2 = mem_block(xf, g_mem_attn[i], w_mq[i].astype(BF16), km, vm, w_mo[i].astype(BF16),
                           g_ffn[i], seq=seq)

        u = matmul(h2, w_up, layer=i, bm=bm, bn=1024, out_dtype=BF16, kind="relu2")
        xf = matmul(u, w_down, layer=i, bm=bm, bn=1024, bk=4096, bn_first=1024, bk_first=2048,
                    out_dtype=F32, kind="residual", res=xf)

    return rmsnorm(xf, g_final, F32).reshape(bsz, seq, d)
```

```python
import functools
import math

import jax
import jax.numpy as jnp
from jax import lax
from jax.experimental import pallas as pl
from jax.experimental.pallas import tpu as pltpu

CHUNK = 64
HEAD_DIM = 128
KV_GROUP = 4
IDX_DIM = 128
TOPK_MAX = 256
LEFT_CHUNKS = 8
REL_CLIP = 256
MEM_HEADS = 4
MEM_HEAD_DIM = 128
ROPE_THETA = 10000.0
EPS = 1e-6

LANES = 128
V7X_VMEM_BYTES = 64 * 1024 * 1024
VMEM_LIMIT = V7X_VMEM_BYTES - 8 * 1024 * 1024

RES_DOT_CHUNK = 512

LOG2E = 1.4426950408889634
NEG_BIG = -1e30
INT_MIN = -(2 ** 31)

F32 = jnp.float32
BF16 = jnp.bfloat16
NT_DIMS = (((1,), (1,)), ((), ()))


def _cparams(sem):
    return pltpu.CompilerParams(dimension_semantics=sem, vmem_limit_bytes=VMEM_LIMIT)


def _rms(x, g):
    return x * lax.rsqrt(jnp.mean(x * x, axis=-1, keepdims=True) + EPS) * g


def _rmsnorm_kernel(x_ref, g_ref, o_ref):
    o_ref[...] = _rms(x_ref[...], g_ref[...]).astype(o_ref.dtype)


def rmsnorm(x, g, out_dtype, tm=256):
    m, d = x.shape
    return pl.pallas_call(
        _rmsnorm_kernel,
        grid=(m // tm,),
        in_specs=[pl.BlockSpec((tm, d), lambda i: (i, 0)),
                  pl.BlockSpec((1, d), lambda i: (0, 0))],
        out_specs=pl.BlockSpec((tm, d), lambda i: (i, 0)),
        out_shape=jax.ShapeDtypeStruct((m, d), out_dtype),
        compiler_params=_cparams(("parallel",)),
        name="rmsnorm",
    )(x, g.reshape(1, d))


def _epilogue(acc, kind, extra_refs, o_ref, scale):
    if kind == "cast":
        o_ref[...] = acc.astype(o_ref.dtype)
    elif kind == "scale":
        o_ref[...] = (acc * scale).astype(o_ref.dtype)
    elif kind == "relu2":
        a = jnp.maximum(acc, 0.0)
        o_ref[...] = (a * a).astype(o_ref.dtype)
    elif kind == "residual":
        o_ref[...] = (extra_refs[0][...] + acc).astype(o_ref.dtype)
    elif kind == "rope":
        cos = extra_refs[0][...]
        sin = extra_refs[1][...]
        for c in range(acc.shape[1] // HEAD_DIM):
            xs = acc[:, c * HEAD_DIM:(c + 1) * HEAD_DIM]
            o_ref[:, c * HEAD_DIM:(c + 1) * HEAD_DIM] = (
                xs * cos + pltpu.roll(xs, HEAD_DIM // 2, 1) * sin).astype(o_ref.dtype)
    else:
        raise ValueError(kind)


def _mm_kernel(a_ref, b_ref, *rest, kind, scale, nk, n_extra, emit_w, w_t, has_prev):
    extra = rest[:n_extra]
    outs = rest[n_extra + (1 if has_prev else 0):]
    o_ref = outs[0]
    w_ref = b_ref
    if emit_w:
        w_ref = outs[1]
        w32 = b_ref[...].T if w_t else b_ref[...]
        w_ref[...] = w32.astype(BF16)
    if kind != "residual":
        acc = jnp.dot(a_ref[...], w_ref[...], preferred_element_type=F32)
        _epilogue(acc, kind, extra, o_ref, scale)
        return
    if nk > 1:
        @pl.when(pl.program_id(2) == 0)
        def _():
            o_ref[...] = extra[0][...]

    bn = o_ref.shape[1]
    cw = min(bn, RES_DOT_CHUNK)
    for c in range(bn // cw):
        cols = slice(c * cw, (c + 1) * cw)
        part = jnp.dot(a_ref[...], w_ref[:, cols], preferred_element_type=F32)
        if nk == 1:
            o_ref[:, cols] = extra[0][:, cols] + part
        else:
            o_ref[:, cols] += part


def _mm_call(a, b, y_prev, *, row0, nrows, col0, n, bm, bn, bk, out_dtype, kind, res, rope, scale,
             emit_w, layer=0, w_t=False):
    m, kdim = a.shape
    nk = kdim // bk
    jb0 = col0 // bn
    assert n % bn == 0 and kdim % bk == 0 and col0 % bn == 0
    assert nk == 1 or (kind == "residual" and out_dtype == F32)
    if w_t:
        assert emit_w and b.ndim == 3
        b_spec = pl.BlockSpec((None, bn, bk), lambda i, j, k: (layer, j + jb0, k))
    elif b.ndim == 3:
        b_spec = pl.BlockSpec((None, bk, bn), lambda i, j, k: (layer, k, j + jb0))
    else:
        b_spec = pl.BlockSpec((bk, bn), lambda i, j, k: (k, j + jb0))
    in_specs = [pl.BlockSpec((bm, bk), lambda i, j, k: (i + row0, k)), b_spec]
    args = [a, b]
    if kind == "residual":
        in_specs.append(pl.BlockSpec((bm, bn), lambda i, j, k: (i + row0, j)))
        args.append(res)
    if kind == "rope":
        cos, sin, seq = rope
        nb = seq // bm
        assert seq % bm == 0
        for t in (cos, sin):
            in_specs.append(pl.BlockSpec((bm, HEAD_DIM), lambda i, j, k: ((i + row0) % nb, 0)))
            args.append(t)
    n_extra = len(args) - 2
    aliases = {}
    if y_prev is not None:
        aliases = {len(args): 0}
        in_specs.append(pl.BlockSpec(memory_space=pl.ANY))
        args.append(y_prev)
    out_specs = [pl.BlockSpec((bm, bn), lambda i, j, k: (i + row0, j))]
    out_shape = [jax.ShapeDtypeStruct((m, n), out_dtype)]
    if emit_w:
        out_specs.append(pl.BlockSpec((bk, bn), lambda i, j, k: (k, j)))
        out_shape.append(jax.ShapeDtypeStruct((kdim, n), BF16))
    return pl.pallas_call(
        functools.partial(_mm_kernel, kind=kind, scale=scale, nk=nk, n_extra=n_extra, emit_w=emit_w,
                          w_t=w_t, has_prev=y_prev is not None),
        grid=(nrows, n // bn, nk),
        in_specs=in_specs,
        out_specs=out_specs,
        out_shape=out_shape,
        input_output_aliases=aliases,
        compiler_params=_cparams(("parallel", "parallel", "arbitrary")),
        name="mm_" + kind + ("_w" if emit_w else ""),
    )(*args)


def matmul(a, w32, *, bm, bn, bk=None, out_dtype, kind="cast", res=None, rope=None, scale=None,
           col0=0, n=None, layer=0, bn_first=512, bk_first=None, w_t=False):
    m, kdim = a.shape
    n = w32.shape[-2 if w_t else -1] if n is None else n
    bk = kdim if bk is None else min(bk, kdim)
    bk_first = bk if bk_first is None else min(bk_first, kdim)
    bm = min(bm, m)
    assert m % bm == 0 and w32.shape[-1 if w_t else -2] == kdim
    common = dict(n=n, bm=bm, out_dtype=out_dtype, kind=kind, res=res, rope=rope, scale=scale)
    y, wb = _mm_call(a, w32, None, row0=0, nrows=1, col0=col0, bn=math.gcd(bn_first, n, col0),
                     bk=bk_first, emit_w=True, layer=layer, w_t=w_t, **common)
    if m > bm:
        y = _mm_call(a, wb, y, row0=1, nrows=m // bm - 1, col0=0, bn=math.gcd(bn, n), bk=bk,
                     emit_w=False, **common)[0]
    return y


def _softmax_pv(t, v, scale):
    m = jnp.max(t, axis=-1, keepdims=True)
    p = jnp.exp2((t - m) * (scale * LOG2E))
    l = jnp.sum(p, axis=-1, keepdims=True)
    o = jnp.dot(p.astype(BF16), v, preferred_element_type=F32)
    return o / l


def _indexer_kernel(qi_ref, ki_ref, wi_ref, o_ref, *, n_idx_heads, q0, topk):
    tq, s_adm = o_ref.shape[1], o_ref.shape[2]
    ki = ki_ref[0]
    wi = wi_ref[...]
    score = jnp.zeros((tq, s_adm), F32)
    for h in range(n_idx_heads):
        d = lax.dot_general(qi_ref[:, h * IDX_DIM:(h + 1) * IDX_DIM], ki, NT_DIMS,
                            preferred_element_type=F32)
        score = score + wi[:, h:h + 1] * jnp.maximum(d, 0.0)

    q_chunk = (q0 + lax.broadcasted_iota(jnp.int32, (tq, s_adm), 0)) // CHUNK
    k_chunk = lax.broadcasted_iota(jnp.int32, (tq, s_adm), 1) // CHUNK
    admissible = k_chunk <= q_chunk

    bits = pltpu.bitcast(score, jnp.int32)
    key = jnp.where(bits < 0, bits ^ jnp.int32(0x7FFFFFFF), bits)
    key = jnp.where(admissible, key, jnp.int32(INT_MIN))

    kf = jnp.float32(topk)
    cnt = jnp.sum(jnp.where(key >= 0, 1.0, 0.0), axis=-1, keepdims=True)
    lo0 = jnp.where(cnt >= kf, jnp.int32(0), jnp.int32(INT_MIN))

    def body(i, lo):
        cand = lo + (jnp.int32(1) << (30 - i))
        c = jnp.sum(jnp.where(key >= cand, 1.0, 0.0), axis=-1, keepdims=True)
        return jnp.where(c >= kf, cand, lo)

    thr = lax.fori_loop(0, 31, body, lo0)
    sel = jnp.logical_and(admissible, key >= thr)
    o_ref[0] = jnp.where(sel, 0.0, NEG_BIG).astype(o_ref.dtype)

    n_sel = jnp.sum(jnp.where(sel, 1.0, 0.0), axis=-1, keepdims=True)

    @pl.when(jnp.max(n_sel) > kf)
    def _():
        gt = key > thr
        eq = jnp.logical_and(admissible, key == thr)
        need = kf - jnp.sum(jnp.where(gt, 1.0, 0.0), axis=-1, keepdims=True)
        idx = lax.broadcasted_iota(jnp.int32, (tq, s_adm), 1)
        nbits = max(1, (s_adm - 1).bit_length())

        def tie_body(i, last):
            cand = last + (jnp.int32(1) << (nbits - 1 - i))
            c = jnp.sum(jnp.where(jnp.logical_and(eq, idx <= cand), 1.0, 0.0), axis=-1, keepdims=True)
            return jnp.where(c < need, cand, last)

        last = lax.fori_loop(0, nbits, tie_body, jnp.full((tq, 1), -1, jnp.int32))
        keep = jnp.logical_or(gt, jnp.logical_and(eq, idx <= last + 1))
        o_ref[0] = jnp.where(keep, 0.0, NEG_BIG).astype(o_ref.dtype)


def indexer_mask(qi, ki3, wi, *, bsz, seq, qb, tq, topk):
    n_idx_heads = qi.shape[1] // IDX_DIM
    s_adm = (qb + 1) * tq
    nqb = seq // tq
    return pl.pallas_call(
        functools.partial(_indexer_kernel, n_idx_heads=n_idx_heads, q0=qb * tq, topk=topk),
        grid=(bsz,),
        in_specs=[pl.BlockSpec((tq, qi.shape[1]), lambda b: (b * nqb + qb, 0)),
                  pl.BlockSpec((1, s_adm, IDX_DIM), lambda b: (b, 0, 0)),
                  pl.BlockSpec((tq, LANES), lambda b: (b * nqb + qb, 0))],
        out_specs=pl.BlockSpec((1, tq, s_adm), lambda b: (b, 0, 0)),
        out_shape=jax.ShapeDtypeStruct((bsz, tq, s_adm), F32),
        compiler_params=_cparams(("parallel",)),
        name="dsa_indexer",
    )(qi, ki3, wi)


DSA_HEADS_PER_STEP = 2


def _dsa_attn_kernel(q_ref, k_ref, v_ref, *rest, tq):
    mask_refs, o_ref = rest[:-1], rest[-1]
    for qb, m_ref in enumerate(mask_refs):
        s_adm = (qb + 1) * tq
        rows = slice(qb * tq, (qb + 1) * tq)
        k = k_ref[0, :s_adm, :]
        v = v_ref[0, :s_adm, :]
        mask = m_ref[0]
        for j in range(DSA_HEADS_PER_STEP):
            cols = slice(j * HEAD_DIM, (j + 1) * HEAD_DIM)
            t = lax.dot_general(q_ref[rows, cols], k, NT_DIMS, preferred_element_type=F32) + mask
            o_ref[rows, cols] = _softmax_pv(t, v, HEAD_DIM ** -0.5).astype(o_ref.dtype)


def dsa_attention(q, k3, v3, masks, *, bsz, seq, tq):
    m, d_q = q.shape
    n_kv = k3.shape[2] // HEAD_DIM
    hw = DSA_HEADS_PER_STEP * HEAD_DIM
    per_group = KV_GROUP // DSA_HEADS_PER_STEP
    q_blk = lambda b, g, j: (b, g * per_group + j)
    kv_blk = lambda b, g, j: (b, 0, g)
    in_specs = [pl.BlockSpec((seq, hw), q_blk),
                pl.BlockSpec((1, seq, HEAD_DIM), kv_blk),
                pl.BlockSpec((1, seq, HEAD_DIM), kv_blk)]
    in_specs += [pl.BlockSpec((1, tq, mk.shape[2]), lambda b, g, j: (b, 0, 0)) for mk in masks]
    return pl.pallas_call(
        functools.partial(_dsa_attn_kernel, tq=tq),
        grid=(bsz, n_kv, per_group),
        in_specs=in_specs,
        out_specs=pl.BlockSpec((seq, hw), q_blk),
        out_shape=jax.ShapeDtypeStruct((m, d_q), BF16),
        compiler_params=_cparams(("parallel", "parallel", "parallel")),
        name="dsa_attn",
    )(q, k3, v3, *masks)


def _band_attn_kernel(q_ref, k_ref, v_ref, g_ref, o_ref, bias_ref, *, tq, pad):
    seq = q_ref.shape[1]
    width = tq + pad
    scale = HEAD_DIM ** -0.5

    @pl.when(pl.program_id(1) == 0)
    def _():
        lp = g_ref.shape[2]
        rows = jnp.broadcast_to(g_ref[0], (tq, lp))
        toe = pltpu.roll(rows, lp - tq + 1, 1, stride=1, stride_axis=0)[:, :width]
        tc = lax.broadcasted_iota(jnp.int32, (tq, width), 0) // CHUNK
        jc = lax.broadcasted_iota(jnp.int32, (tq, width), 1) // CHUNK
        in_band = jnp.logical_and(jc >= tc, jc <= tc + LEFT_CHUNKS)
        bias_ref[...] = jnp.where(in_band, toe * (1.0 / scale), NEG_BIG)

    for qb in range(seq // tq):
        q0 = qb * tq
        klo = max(0, q0 - pad)
        nk = q0 + tq - klo
        t = lax.dot_general(q_ref[0, q0:q0 + tq, :], k_ref[0, klo:klo + nk, :], NT_DIMS,
                            preferred_element_type=F32) + bias_ref[:, width - nk:]
        o_ref[0, q0:q0 + tq, :] = _softmax_pv(t, v_ref[0, klo:klo + nk, :], scale).astype(o_ref.dtype)


def band_attention(q3, k3, v3, rel_rows, *, tq):
    bsz, seq, d = q3.shape
    n_heads = d // HEAD_DIM
    pad = LEFT_CHUNKS * CHUNK
    blk = lambda h, b: (b, 0, h)
    return pl.pallas_call(
        functools.partial(_band_attn_kernel, tq=tq, pad=pad),
        grid=(n_heads, bsz),
        in_specs=[pl.BlockSpec((1, seq, HEAD_DIM), blk),
                  pl.BlockSpec((1, seq, HEAD_DIM), blk),
                  pl.BlockSpec((1, seq, HEAD_DIM), blk),
                  pl.BlockSpec((1, 1, rel_rows.shape[2]), lambda h, b: (h, 0, 0))],
        out_specs=pl.BlockSpec((1, seq, HEAD_DIM), blk),
        out_shape=jax.ShapeDtypeStruct((bsz, seq, d), BF16),
        scratch_shapes=[pltpu.VMEM((tq, tq + pad), F32)],
        compiler_params=_cparams(("parallel", "arbitrary")),
        name="band_attn",
    )(q3, k3, v3, rel_rows)


def band_rel_rows(rel_bias, tq):
    pad = LEFT_CHUNKS * CHUNK
    lp = -(-(2 * tq + pad - 1) // LANES) * LANES
    rel = tq + pad - 1 - jnp.arange(lp)
    idx = jnp.clip(rel, -REL_CLIP, REL_CLIP) + REL_CLIP
    return rel_bias[:, idx].astype(F32)[:, None, :]


def _mem_block_kernel(x_ref, g1_ref, wq_ref, k_ref, v_ref, wo_ref, g2_ref, x_out_ref, h_out_ref):
    x = x_ref[...]
    h = _rms(x, g1_ref[...]).astype(BF16)
    q = jnp.dot(h, wq_ref[...], preferred_element_type=F32).astype(BF16)
    scale = MEM_HEAD_DIM ** -0.5
    outs = []
    for hd in range(MEM_HEADS):
        sl = slice(hd * MEM_HEAD_DIM, (hd + 1) * MEM_HEAD_DIM)
        t = lax.dot_general(q[:, sl], k_ref[0, :, sl], NT_DIMS, preferred_element_type=F32)
        outs.append(_softmax_pv(t, v_ref[0, :, sl], scale).astype(BF16))
    o = jnp.concatenate(outs, axis=-1)
    y = x + jnp.dot(o, wo_ref[...], preferred_element_type=F32)
    x_out_ref[...] = y
    h_out_ref[...] = _rms(y, g2_ref[...]).astype(h_out_ref.dtype)


def mem_block(x, g1, wq, km3, vm3, wo, g2, *, seq, tm=512):
    m, d = x.shape
    md = wq.shape[1]
    n_mem = km3.shape[1]
    per_b = seq // tm
    const2 = lambda i: (0, 0)
    once = pl.Buffered(1)
    return pl.pallas_call(
        _mem_block_kernel,
        grid=(m // tm,),
        in_specs=[pl.BlockSpec((tm, d), lambda i: (i, 0)),
                  pl.BlockSpec((1, d), const2, pipeline_mode=once),
                  pl.BlockSpec((d, md), const2, pipeline_mode=once),
                  pl.BlockSpec((1, n_mem, md), lambda i: (i // per_b, 0, 0)),
                  pl.BlockSpec((1, n_mem, md), lambda i: (i // per_b, 0, 0)),
                  pl.BlockSpec((md, d), const2, pipeline_mode=once),
                  pl.BlockSpec((1, d), const2, pipeline_mode=once)],
        out_specs=[pl.BlockSpec((tm, d), lambda i: (i, 0)),
                   pl.BlockSpec((tm, d), lambda i: (i, 0))],
        out_shape=[jax.ShapeDtypeStruct((m, d), F32),
                   jax.ShapeDtypeStruct((m, d), BF16)],
        compiler_params=_cparams(("parallel",)),
        name="mem_block",
    )(x, g1.reshape(1, d), wq, km3, vm3, wo, g2.reshape(1, d))


def _rope_tables(seq):
    half = HEAD_DIM // 2
    inv = ROPE_THETA ** (-jnp.arange(half, dtype=F32) / half)
    ang = jnp.arange(seq, dtype=F32)[:, None] * inv[None, :]
    cos, sin = jnp.cos(ang), jnp.sin(ang)
    return jnp.concatenate([cos, cos], axis=-1), jnp.concatenate([-sin, sin], axis=-1)


def _pad_cols(w, n):
    return jnp.pad(w, ((0, 0), (0, n - w.shape[1])))


def kernel(x, mem, g_mix, g_mem_attn, g_ffn, g_memory, g_final, w_in_a, w_out_a, w_in_b, rel_bias,
           w_out_b, w_mq, w_mk, w_mv, w_mo, w_up, w_down):
    bsz, seq, d = x.shape
    depth = g_mix.shape[0]
    m = bsz * seq
    n_heads = d // HEAD_DIM
    n_kv = n_heads // KV_GROUP
    a_q, a_kv, a_iq = n_heads * HEAD_DIM, n_kv * HEAD_DIM, n_heads * IDX_DIM
    n_idx_heads = n_heads
    topk = min(TOPK_MAX, seq // 4)
    tq = 256
    bm = 1024

    xf = x.reshape(m, d)
    cos, sin = _rope_tables(seq)
    rope = (cos, sin, seq)

    n_mem = mem.shape[1]
    mem_n = rmsnorm(mem.reshape(bsz * n_mem, d), g_memory, BF16)

    for i in range(depth):
        j = i // 2
        h = rmsnorm(xf, g_mix[i], BF16)
        if i % 2 == 0:
            w = w_in_a
            c0, c1, c2, c3, c4 = a_q, a_q + a_kv, a_q + 2 * a_kv, a_q + 2 * a_kv + a_iq, a_q + 2 * a_kv + a_iq + IDX_DIM
            mm = functools.partial(matmul, h, jnp.swapaxes(w, 1, 2), w_t=True, layer=j, bm=bm, bn=1024)
            q = mm(col0=0, n=a_q, out_dtype=BF16, kind="rope", rope=rope)
            k = mm(col0=c0, n=a_kv, out_dtype=BF16, kind="rope", rope=rope)
            v = mm(col0=c1, n=a_kv, out_dtype=BF16)
            qi = mm(col0=c2, n=a_iq, out_dtype=BF16, kind="rope", rope=rope)
            ki = mm(col0=c3, n=IDX_DIM, out_dtype=BF16, kind="rope", rope=rope)
            wi = matmul(h, _pad_cols(w[j, :, c4:], LANES), bm=bm, bn=LANES, out_dtype=F32, kind="scale",
                        scale=(n_idx_heads ** -0.5) * (IDX_DIM ** -0.5))
            k3 = k.reshape(bsz, seq, a_kv)
            v3 = v.reshape(bsz, seq, a_kv)
            ki3 = ki.reshape(bsz, seq, IDX_DIM)
            masks = [indexer_mask(qi, ki3, wi, bsz=bsz, seq=seq, qb=qb, tq=tq, topk=topk)
                     for qb in range(seq // tq)]
            o = dsa_attention(q, k3, v3, masks, bsz=bsz, seq=seq, tq=tq)
            xf = matmul(o, w_out_a, layer=j, bm=bm, bn=1024, out_dtype=F32, kind="residual", res=xf)
        else:
            qkv = [matmul(h, w_in_b, layer=j, col0=t * d, n=d, bm=bm, bn=1024,
                          out_dtype=BF16).reshape(bsz, seq, d) for t in range(3)]
            o = band_attention(*qkv, band_rel_rows(rel_bias[j], tq), tq=tq).reshape(m, d)
            xf = matmul(o, w_out_b, layer=j, bm=bm, bn=1024, out_dtype=F32, kind="residual", res=xf)

        md = w_mq.shape[2]
        km = matmul(mem_n, w_mk, layer=i, bm=bm, bn=md, out_dtype=BF16).reshape(bsz, n_mem, md)
        vm = matmul(mem_n, w_mv, layer=i, bm=bm, bn=md, out_dtype=BF16).reshape(bsz, n_mem, md)
        xf, h2 = mem_block(xf, g_mem_attn[i], w_mq[i].astype(BF16), km, vm, w_mo[i].astype(BF16),
                           g_ffn[i], seq=seq)

        u = matmul(h2, w_up, layer=i, bm=bm, bn=1024, out_dtype=BF16, kind="relu2")
        xf = matmul(u, w_down, layer=i, bm=bm, bn=1024, bk=4096, bn_first=1024, bk_first=2048,
                    out_dtype=F32, kind="residual", res=xf)

    return rmsnorm(xf, g_final, F32).reshape(bsz, seq, d)
```

```python
import functools
import math

import jax
import jax.numpy as jnp
from jax import lax
from jax.experimental import pallas as pl
from jax.experimental.pallas import tpu as pltpu

CHUNK = 64
HEAD_DIM = 128
KV_GROUP = 4
IDX_DIM = 128
TOPK_MAX = 256
LEFT_CHUNKS = 8
REL_CLIP = 256
MEM_HEADS = 4
MEM_HEAD_DIM = 128
ROPE_THETA = 10000.0
EPS = 1e-6

LANES = 128
V7X_VMEM_BYTES = 64 * 1024 * 1024
VMEM_LIMIT = V7X_VMEM_BYTES - 8 * 1024 * 1024

RES_DOT_CHUNK = 512

LOG2E = 1.4426950408889634
NEG_BIG = -1e30
INT_MIN = -(2 ** 31)

F32 = jnp.float32
BF16 = jnp.bfloat16
NT_DIMS = (((1,), (1,)), ((), ()))


def _cparams(sem):
    return pltpu.CompilerParams(dimension_semantics=sem, vmem_limit_bytes=VMEM_LIMIT)


def _rms(x, g):
    return x * lax.rsqrt(jnp.mean(x * x, axis=-1, keepdims=True) + EPS) * g


def _rmsnorm_kernel(x_ref, g_ref, o_ref):
    o_ref[...] = _rms(x_ref[...], g_ref[...]).astype(o_ref.dtype)


def rmsnorm(x, g, out_dtype, tm=256):
    m, d = x.shape
    return pl.pallas_call(
        _rmsnorm_kernel,
        grid=(m // tm,),
        in_specs=[pl.BlockSpec((tm, d), lambda i: (i, 0)),
                  pl.BlockSpec((1, d), lambda i: (0, 0))],
        out_specs=pl.BlockSpec((tm, d), lambda i: (i, 0)),
        out_shape=jax.ShapeDtypeStruct((m, d), out_dtype),
        compiler_params=_cparams(("parallel",)),
        name="rmsnorm",
    )(x, g.reshape(1, d))


def _epilogue(acc, kind, extra_refs, o_ref, scale, plain_cols=None):
    if kind == "cast":
        o_ref[...] = acc.astype(o_ref.dtype)
    elif kind == "scale":
        o_ref[...] = (acc * scale).astype(o_ref.dtype)
    elif kind == "relu2":
        a = jnp.maximum(acc, 0.0)
        o_ref[...] = (a * a).astype(o_ref.dtype)
    elif kind == "residual":
        o_ref[...] = (extra_refs[0][...] + acc).astype(o_ref.dtype)
    elif kind == "rope":
        cos = extra_refs[0][...]
        sin = extra_refs[1][...]
        bn = acc.shape[1]
        if plain_cols is not None:
            col = pl.program_id(1) * bn
            plain = jnp.logical_and(col >= plain_cols[0], col < plain_cols[1])
        for c in range(bn // HEAD_DIM):
            xs = acc[:, c * HEAD_DIM:(c + 1) * HEAD_DIM]
            y = xs * cos + pltpu.roll(xs, HEAD_DIM // 2, 1) * sin
            if plain_cols is not None:
                y = jnp.where(plain, xs, y)
            o_ref[:, c * HEAD_DIM:(c + 1) * HEAD_DIM] = y.astype(o_ref.dtype)
    else:
        raise ValueError(kind)


def _mm_kernel(a_ref, b_ref, *rest, kind, scale, nk, n_extra, emit_w, w_t, has_prev, plain_cols):
    extra = rest[:n_extra]
    outs = rest[n_extra + (1 if has_prev else 0):]
    o_ref = outs[0]
    w_ref = b_ref
    if emit_w:
        w_ref = outs[1]
        w32 = b_ref[...].T if w_t else b_ref[...]
        w_ref[...] = w32.astype(BF16)
    if kind != "residual":
        acc = jnp.dot(a_ref[...], w_ref[...], preferred_element_type=F32)
        _epilogue(acc, kind, extra, o_ref, scale, plain_cols)
        return
    if nk > 1:
        @pl.when(pl.program_id(2) == 0)
        def _():
            o_ref[...] = extra[0][...]

    bn = o_ref.shape[1]
    cw = min(bn, RES_DOT_CHUNK)
    for c in range(bn // cw):
        cols = slice(c * cw, (c + 1) * cw)
        part = jnp.dot(a_ref[...], w_ref[:, cols], preferred_element_type=F32)
        if nk == 1:
            o_ref[:, cols] = extra[0][:, cols] + part
        else:
            o_ref[:, cols] += part


def _mm_call(a, b, y_prev, *, row0, nrows, col0, n, bm, bn, bk, out_dtype, kind, res, rope, scale,
             emit_w, layer=0, w_t=False):
    m, kdim = a.shape
    nk = kdim // bk
    jb0 = col0 // bn
    assert n % bn == 0 and kdim % bk == 0 and col0 % bn == 0
    assert nk == 1 or (kind == "residual" and out_dtype == F32)
    if w_t:
        assert emit_w and b.ndim == 3
        b_spec = pl.BlockSpec((None, bn, bk), lambda i, j, k: (layer, j + jb0, k))
    elif b.ndim == 3:
        b_spec = pl.BlockSpec((None, bk, bn), lambda i, j, k: (layer, k, j + jb0))
    else:
        b_spec = pl.BlockSpec((bk, bn), lambda i, j, k: (k, j + jb0))
    in_specs = [pl.BlockSpec((bm, bk), lambda i, j, k: (i + row0, k)), b_spec]
    args = [a, b]
    if kind == "residual":
        in_specs.append(pl.BlockSpec((bm, bn), lambda i, j, k: (i + row0, j)))
        args.append(res)
    plain_cols = None
    if kind == "rope":
        cos, sin, seq, plain_cols = rope
        nb = seq // bm
        assert seq % bm == 0
        assert plain_cols is None or (plain_cols[0] % bn == 0 and plain_cols[1] % bn == 0)
        for t in (cos, sin):
            in_specs.append(pl.BlockSpec((bm, HEAD_DIM), lambda i, j, k: ((i + row0) % nb, 0)))
            args.append(t)
    n_extra = len(args) - 2
    aliases = {}
    if y_prev is not None:
        aliases = {len(args): 0}
        in_specs.append(pl.BlockSpec(memory_space=pl.ANY))
        args.append(y_prev)
    out_specs = [pl.BlockSpec((bm, bn), lambda i, j, k: (i + row0, j))]
    out_shape = [jax.ShapeDtypeStruct((m, n), out_dtype)]
    if emit_w:
        out_specs.append(pl.BlockSpec((bk, bn), lambda i, j, k: (k, j)))
        out_shape.append(jax.ShapeDtypeStruct((kdim, n), BF16))
    return pl.pallas_call(
        functools.partial(_mm_kernel, kind=kind, scale=scale, nk=nk, n_extra=n_extra, emit_w=emit_w,
                          w_t=w_t, has_prev=y_prev is not None, plain_cols=plain_cols),
        grid=(nrows, n // bn, nk),
        in_specs=in_specs,
        out_specs=out_specs,
        out_shape=out_shape,
        input_output_aliases=aliases,
        compiler_params=_cparams(("parallel", "parallel", "arbitrary")),
        name="mm_" + kind + ("_w" if emit_w else ""),
    )(*args)


def matmul(a, w32, *, bm, bn, bk=None, out_dtype, kind="cast", res=None, rope=None, scale=None,
           col0=0, n=None, layer=0, bn_first=512, bk_first=None, w_t=False):
    m, kdim = a.shape
    n = w32.shape[-2 if w_t else -1] if n is None else n
    bk = kdim if bk is None else min(bk, kdim)
    bk_first = bk if bk_first is None else min(bk_first, kdim)
    bm = min(bm, m)
    assert m % bm == 0 and w32.shape[-1 if w_t else -2] == kdim
    common = dict(n=n, bm=bm, out_dtype=out_dtype, kind=kind, res=res, rope=rope, scale=scale)
    align = math.gcd(n, *(rope[3] if rope is not None and rope[3] is not None else ()))
    y, wb = _mm_call(a, w32, None, row0=0, nrows=1, col0=col0, bn=math.gcd(bn_first, align, col0),
                     bk=bk_first, emit_w=True, layer=layer, w_t=w_t, **common)
    if m > bm:
        y = _mm_call(a, wb, y, row0=1, nrows=m // bm - 1, col0=0, bn=math.gcd(bn, align), bk=bk,
                     emit_w=False, **common)[0]
    return y


def _softmax_pv(t, v, scale):
    m = jnp.max(t, axis=-1, keepdims=True)
    p = jnp.exp2((t - m) * (scale * LOG2E))
    l = jnp.sum(p, axis=-1, keepdims=True)
    o = jnp.dot(p.astype(BF16), v, preferred_element_type=F32)
    return o / l


def _indexer_kernel(qi_lo_ref, qi_hi_ref, ki_ref, wi_ref, o_ref, *, n_idx_heads, q0, topk):
    tq, s_adm = o_ref.shape[1], o_ref.shape[2]
    ki = ki_ref[0]
    wi = wi_ref[...]
    score = jnp.zeros((tq, s_adm), F32)
    half = n_idx_heads // 2
    for h in range(n_idx_heads):
        qi_ref, hh = (qi_lo_ref, h) if h < half else (qi_hi_ref, h - half)
        d = lax.dot_general(qi_ref[:, hh * IDX_DIM:(hh + 1) * IDX_DIM], ki, NT_DIMS,
                            preferred_element_type=F32)
        score = score + wi[:, h:h + 1] * jnp.maximum(d, 0.0)

    q_chunk = (q0 + lax.broadcasted_iota(jnp.int32, (tq, s_adm), 0)) // CHUNK
    k_chunk = lax.broadcasted_iota(jnp.int32, (tq, s_adm), 1) // CHUNK
    admissible = k_chunk <= q_chunk

    bits = pltpu.bitcast(score, jnp.int32)
    key = jnp.where(bits < 0, bits ^ jnp.int32(0x7FFFFFFF), bits)
    key = jnp.where(admissible, key, jnp.int32(INT_MIN))

    kf = jnp.float32(topk)
    cnt = jnp.sum(jnp.where(key >= 0, 1.0, 0.0), axis=-1, keepdims=True)
    lo0 = jnp.where(cnt >= kf, jnp.int32(0), jnp.int32(INT_MIN))

    def body(i, lo):
        cand = lo + (jnp.int32(1) << (30 - i))
        c = jnp.sum(jnp.where(key >= cand, 1.0, 0.0), axis=-1, keepdims=True)
        return jnp.where(c >= kf, cand, lo)

    thr = lax.fori_loop(0, 31, body, lo0)
    sel = jnp.logical_and(admissible, key >= thr)
    o_ref[0] = jnp.where(sel, 0.0, NEG_BIG).astype(o_ref.dtype)

    n_sel = jnp.sum(jnp.where(sel, 1.0, 0.0), axis=-1, keepdims=True)

    @pl.when(jnp.max(n_sel) > kf)
    def _():
        gt = key > thr
        eq = jnp.logical_and(admissible, key == thr)
        need = kf - jnp.sum(jnp.where(gt, 1.0, 0.0), axis=-1, keepdims=True)
        idx = lax.broadcasted_iota(jnp.int32, (tq, s_adm), 1)
        nbits = max(1, (s_adm - 1).bit_length())

        def tie_body(i, last):
            cand = last + (jnp.int32(1) << (nbits - 1 - i))
            c = jnp.sum(jnp.where(jnp.logical_and(eq, idx <= cand), 1.0, 0.0), axis=-1, keepdims=True)
            return jnp.where(c < need, cand, last)

        last = lax.fori_loop(0, nbits, tie_body, jnp.full((tq, 1), -1, jnp.int32))
        keep = jnp.logical_or(gt, jnp.logical_and(eq, idx <= last + 1))
        o_ref[0] = jnp.where(keep, 0.0, NEG_BIG).astype(o_ref.dtype)


def indexer_mask(proj, ki3, wi, *, qi_col0, n_idx_heads, bsz, seq, qb, tq, topk):
    s_adm = (qb + 1) * tq
    nqb = seq // tq
    half_w = n_idx_heads // 2 * IDX_DIM
    assert n_idx_heads % 2 == 0 and qi_col0 % half_w == 0
    cb = qi_col0 // half_w
    return pl.pallas_call(
        functools.partial(_indexer_kernel, n_idx_heads=n_idx_heads, q0=qb * tq, topk=topk),
        grid=(bsz,),
        in_specs=[pl.BlockSpec((tq, half_w), lambda b: (b * nqb + qb, cb)),
                  pl.BlockSpec((tq, half_w), lambda b: (b * nqb + qb, cb + 1)),
                  pl.BlockSpec((1, s_adm, IDX_DIM), lambda b: (b, 0, 0)),
                  pl.BlockSpec((tq, LANES), lambda b: (b * nqb + qb, 0))],
        out_specs=pl.BlockSpec((1, tq, s_adm), lambda b: (b, 0, 0)),
        out_shape=jax.ShapeDtypeStruct((bsz, tq, s_adm), F32),
        compiler_params=_cparams(("parallel",)),
        name="dsa_indexer",
    )(proj, proj, ki3, wi)


DSA_HEADS_PER_STEP = 2


def _dsa_attn_kernel(q_ref, k_ref, v_ref, *rest, tq):
    mask_refs, o_ref = rest[:-1], rest[-1]
    for qb, m_ref in enumerate(mask_refs):
        s_adm = (qb + 1) * tq
        rows = slice(qb * tq, (qb + 1) * tq)
        k = k_ref[0, :s_adm, :]
        v = v_ref[0, :s_adm, :]
        mask = m_ref[0]
        for j in range(DSA_HEADS_PER_STEP):
            cols = slice(j * HEAD_DIM, (j + 1) * HEAD_DIM)
            t = lax.dot_general(q_ref[rows, cols], k, NT_DIMS, preferred_element_type=F32) + mask
            o_ref[rows, cols] = _softmax_pv(t, v, HEAD_DIM ** -0.5).astype(o_ref.dtype)


def dsa_attention(proj, masks, *, n_heads, bsz, seq, tq):
    m = proj.shape[0]
    n_kv = n_heads // KV_GROUP
    proj3 = proj.reshape(bsz, seq, proj.shape[1])
    hw = DSA_HEADS_PER_STEP * HEAD_DIM
    per_group = KV_GROUP // DSA_HEADS_PER_STEP
    q_blk = lambda b, g, j: (b, g * per_group + j)
    in_specs = [pl.BlockSpec((seq, hw), q_blk),
                pl.BlockSpec((1, seq, HEAD_DIM), lambda b, g, j: (b, 0, n_heads + g)),
                pl.BlockSpec((1, seq, HEAD_DIM), lambda b, g, j: (b, 0, n_heads + n_kv + g))]
    in_specs += [pl.BlockSpec((1, tq, mk.shape[2]), lambda b, g, j: (b, 0, 0)) for mk in masks]
    return pl.pallas_call(
        functools.partial(_dsa_attn_kernel, tq=tq),
        grid=(bsz, n_kv, per_group),
        in_specs=in_specs,
        out_specs=pl.BlockSpec((seq, hw), q_blk),
        out_shape=jax.ShapeDtypeStruct((m, n_heads * HEAD_DIM), BF16),
        compiler_params=_cparams(("parallel", "parallel", "parallel")),
        name="dsa_attn",
    )(proj, proj3, proj3, *masks)


def _band_attn_kernel(q_ref, k_ref, v_ref, g_ref, o_ref, bias_ref, *, tq, pad):
    seq = q_ref.shape[1]
    width = tq + pad
    scale = HEAD_DIM ** -0.5

    @pl.when(pl.program_id(1) == 0)
    def _():
        lp = g_ref.shape[2]
        rows = jnp.broadcast_to(g_ref[0], (tq, lp))
        toe = pltpu.roll(rows, lp - tq + 1, 1, stride=1, stride_axis=0)[:, :width]
        tc = lax.broadcasted_iota(jnp.int32, (tq, width), 0) // CHUNK
        jc = lax.broadcasted_iota(jnp.int32, (tq, width), 1) // CHUNK
        in_band = jnp.logical_and(jc >= tc, jc <= tc + LEFT_CHUNKS)
        bias_ref[...] = jnp.where(in_band, toe * (1.0 / scale), NEG_BIG)

    for qb in range(seq // tq):
        q0 = qb * tq
        klo = max(0, q0 - pad)
        nk = q0 + tq - klo
        t = lax.dot_general(q_ref[0, q0:q0 + tq, :], k_ref[0, klo:klo + nk, :], NT_DIMS,
                            preferred_element_type=F32) + bias_ref[:, width - nk:]
        o_ref[0, q0:q0 + tq, :] = _softmax_pv(t, v_ref[0, klo:klo + nk, :], scale).astype(o_ref.dtype)


def band_attention(qkv3, rel_rows, *, tq):
    bsz, seq, d3 = qkv3.shape
    n_heads = d3 // (3 * HEAD_DIM)
    pad = LEFT_CHUNKS * CHUNK
    return pl.pallas_call(
        functools.partial(_band_attn_kernel, tq=tq, pad=pad),
        grid=(n_heads, bsz),
        in_specs=[pl.BlockSpec((1, seq, HEAD_DIM), lambda h, b: (b, 0, h)),
                  pl.BlockSpec((1, seq, HEAD_DIM), lambda h, b: (b, 0, n_heads + h)),
                  pl.BlockSpec((1, seq, HEAD_DIM), lambda h, b: (b, 0, 2 * n_heads + h)),
                  pl.BlockSpec((1, 1, rel_rows.shape[2]), lambda h, b: (h, 0, 0))],
        out_specs=pl.BlockSpec((1, seq, HEAD_DIM), lambda h, b: (b, 0, h)),
        out_shape=jax.ShapeDtypeStruct((bsz, seq, n_heads * HEAD_DIM), BF16),
        scratch_shapes=[pltpu.VMEM((tq, tq + pad), F32)],
        compiler_params=_cparams(("parallel", "arbitrary")),
        name="band_attn",
    )(qkv3, qkv3, qkv3, rel_rows)


def band_rel_rows(rel_bias, tq):
    pad = LEFT_CHUNKS * CHUNK
    lp = -(-(2 * tq + pad - 1) // LANES) * LANES
    rel = tq + pad - 1 - jnp.arange(lp)
    idx = jnp.clip(rel, -REL_CLIP, REL_CLIP) + REL_CLIP
    return rel_bias[:, idx].astype(F32)[:, None, :]


def _mem_block_kernel(x_ref, g1_ref, wq_ref, k_ref, v_ref, wo_ref, g2_ref, x_out_ref, h_out_ref):
    x = x_ref[...]
    h = _rms(x, g1_ref[...]).astype(BF16)
    q = jnp.dot(h, wq_ref[...], preferred_element_type=F32).astype(BF16)
    scale = MEM_HEAD_DIM ** -0.5
    outs = []
    for hd in range(MEM_HEADS):
        sl = slice(hd * MEM_HEAD_DIM, (hd + 1) * MEM_HEAD_DIM)
        t = lax.dot_general(q[:, sl], k_ref[0, :, sl], NT_DIMS, preferred_element_type=F32)
        outs.append(_softmax_pv(t, v_ref[0, :, sl], scale).astype(BF16))
    o = jnp.concatenate(outs, axis=-1)
    y = x + jnp.dot(o, wo_ref[...], preferred_element_type=F32)
    x_out_ref[...] = y
    h_out_ref[...] = _rms(y, g2_ref[...]).astype(h_out_ref.dtype)


def mem_block(x, g1, wq, km3, vm3, wo, g2, *, seq, tm=512):
    m, d = x.shape
    md = wq.shape[1]
    n_mem = km3.shape[1]
    per_b = seq // tm
    const2 = lambda i: (0, 0)
    once = pl.Buffered(1)
    return pl.pallas_call(
        _mem_block_kernel,
        grid=(m // tm,),
        in_specs=[pl.BlockSpec((tm, d), lambda i: (i, 0)),
                  pl.BlockSpec((1, d), const2, pipeline_mode=once),
                  pl.BlockSpec((d, md), const2, pipeline_mode=once),
                  pl.BlockSpec((1, n_mem, md), lambda i: (i // per_b, 0, 0)),
                  pl.BlockSpec((1, n_mem, md), lambda i: (i // per_b, 0, 0)),
                  pl.BlockSpec((md, d), const2, pipeline_mode=once),
                  pl.BlockSpec((1, d), const2, pipeline_mode=once)],
        out_specs=[pl.BlockSpec((tm, d), lambda i: (i, 0)),
                   pl.BlockSpec((tm, d), lambda i: (i, 0))],
        out_shape=[jax.ShapeDtypeStruct((m, d), F32),
                   jax.ShapeDtypeStruct((m, d), BF16)],
        compiler_params=_cparams(("parallel",)),
        name="mem_block",
    )(x, g1.reshape(1, d), wq, km3, vm3, wo, g2.reshape(1, d))


def _rope_tables(seq):
    half = HEAD_DIM // 2
    inv = ROPE_THETA ** (-jnp.arange(half, dtype=F32) / half)
    ang = jnp.arange(seq, dtype=F32)[:, None] * inv[None, :]
    cos, sin = jnp.cos(ang), jnp.sin(ang)
    return jnp.concatenate([cos, cos], axis=-1), jnp.concatenate([-sin, sin], axis=-1)


def _pad_cols(w, n):
    return jnp.pad(w, ((0, 0), (0, n - w.shape[1])))


def kernel(x, mem, g_mix, g_mem_attn, g_ffn, g_memory, g_final, w_in_a, w_out_a, w_in_b, rel_bias,
           w_out_b, w_mq, w_mk, w_mv, w_mo, w_up, w_down):
    bsz, seq, d = x.shape
    depth = g_mix.shape[0]
    m = bsz * seq
    n_heads = d // HEAD_DIM
    n_kv = n_heads // KV_GROUP
    a_q, a_kv, a_iq = n_heads * HEAD_DIM, n_kv * HEAD_DIM, n_heads * IDX_DIM
    n_idx_heads = n_heads
    topk = min(TOPK_MAX, seq // 4)
    tq = 256
    bm = 1024

    xf = x.reshape(m, d)
    cos, sin = _rope_tables(seq)

    n_mem = mem.shape[1]
    mem_n = rmsnorm(mem.reshape(bsz * n_mem, d), g_memory, BF16)

    for i in range(depth):
        j = i // 2
        h = rmsnorm(xf, g_mix[i], BF16)
        if i % 2 == 0:
            w = w_in_a
            c0, c1, c2, c3, c4 = a_q, a_q + a_kv, a_q + 2 * a_kv, a_q + 2 * a_kv + a_iq, a_q + 2 * a_kv + a_iq + IDX_DIM
            mm = functools.partial(matmul, h, jnp.swapaxes(w, 1, 2), w_t=True, layer=j, bm=bm, bn=1024,
                                   out_dtype=BF16, kind="rope")
            proj = mm(col0=0, n=c3, rope=(cos, sin, seq, (c1, c2)))
            ki = mm(col0=c3, n=IDX_DIM, rope=(cos, sin, seq, None))
            wi = matmul(h, _pad_cols(w[j, :, c4:], LANES), bm=bm, bn=LANES, out_dtype=F32, kind="scale",
                        scale=(n_idx_heads ** -0.5) * (IDX_DIM ** -0.5))
            ki3 = ki.reshape(bsz, seq, IDX_DIM)
            masks = [indexer_mask(proj, ki3, wi, qi_col0=c2, n_idx_heads=n_idx_heads, bsz=bsz, seq=seq,
                                  qb=qb, tq=tq, topk=topk) for qb in range(seq // tq)]
            o = dsa_attention(proj, masks, n_heads=n_heads, bsz=bsz, seq=seq, tq=tq)
            xf = matmul(o, w_out_a, layer=j, bm=bm, bn=1024, out_dtype=F32, kind="residual", res=xf)
        else:
            qkv = matmul(h, w_in_b, layer=j, bm=bm, bn=1024, out_dtype=BF16)
            o = band_attention(qkv.reshape(bsz, seq, 3 * d), band_rel_rows(rel_bias[j], tq), tq=tq)
            xf = matmul(o.reshape(m, d), w_out_b, layer=j, bm=bm, bn=1024, out_dtype=F32,
                        kind="residual", res=xf)

        md = w_mq.shape[2]
        km = matmul(mem_n, w_mk, layer=i, bm=bm, bn=md, out_dtype=BF16).reshape(bsz, n_mem, md)
        vm = matmul(mem_n, w_mv, layer=i, bm=bm, bn=md, out_dtype=BF16).reshape(bsz, n_mem, md)
        xf, h2 = mem_block(xf, g_mem_attn[i], w_mq[i].astype(BF16), km, vm, w_mo[i].astype(BF16),
                           g_ffn[i], seq=seq)

        u = matmul(h2, w_up, layer=i, bm=bm, bn=1024, out_dtype=BF16, kind="relu2")
        xf = matmul(u, w_down, layer=i, bm=bm, bn=1024, bk=4096, bn_first=1024, bk_first=2048,
                    out_dtype=F32, kind="residual", res=xf)

    return rmsnorm(xf, g_final, F32).reshape(bsz, seq, d)
```

```python
import functools
import math

import jax
import jax.numpy as jnp
from jax import lax
from jax.experimental import pallas as pl
from jax.experimental.pallas import tpu as pltpu

CHUNK = 64
HEAD_DIM = 128
KV_GROUP = 4
IDX_DIM = 128
TOPK_MAX = 256
LEFT_CHUNKS = 8
REL_CLIP = 256
MEM_HEADS = 4
MEM_HEAD_DIM = 128
ROPE_THETA = 10000.0
EPS = 1e-6

LANES = 128
V7X_VMEM_BYTES = 64 * 1024 * 1024
VMEM_LIMIT = V7X_VMEM_BYTES - 8 * 1024 * 1024

RES_DOT_CHUNK = 512

LOG2E = 1.4426950408889634
NEG_BIG = -1e30
INT_MIN = -(2 ** 31)

F32 = jnp.float32
BF16 = jnp.bfloat16
NT_DIMS = (((1,), (1,)), ((), ()))


def _cparams(sem):
    return pltpu.CompilerParams(dimension_semantics=sem, vmem_limit_bytes=VMEM_LIMIT)


def _rms(x, g):
    return x * lax.rsqrt(jnp.mean(x * x, axis=-1, keepdims=True) + EPS) * g


def _rmsnorm_kernel(x_ref, g_ref, o_ref):
    o_ref[...] = _rms(x_ref[...], g_ref[...]).astype(o_ref.dtype)


def rmsnorm(x, g, out_dtype, tm=512):
    m, d = x.shape
    tm = min(tm, m)
    return pl.pallas_call(
        _rmsnorm_kernel,
        grid=(m // tm,),
        in_specs=[pl.BlockSpec((tm, d), lambda i: (i, 0)),
                  pl.BlockSpec((1, d), lambda i: (0, 0))],
        out_specs=pl.BlockSpec((tm, d), lambda i: (i, 0)),
        out_shape=jax.ShapeDtypeStruct((m, d), out_dtype),
        compiler_params=_cparams(("parallel",)),
        name="rmsnorm",
    )(x, g.reshape(1, d))


def _epilogue(acc, kind, extra_refs, o_ref, scale, plain_cols=None):
    if kind == "cast":
        o_ref[...] = acc.astype(o_ref.dtype)
    elif kind == "scale":
        o_ref[...] = (acc * scale).astype(o_ref.dtype)
    elif kind == "relu2":
        a = jnp.maximum(acc, 0.0)
        o_ref[...] = (a * a).astype(o_ref.dtype)
    elif kind == "residual":
        o_ref[...] = (extra_refs[0][...] + acc).astype(o_ref.dtype)
    elif kind == "rope":
        cos = extra_refs[0][...]
        sin = extra_refs[1][...]
        bn = acc.shape[1]
        if plain_cols is not None:
            col = pl.program_id(1) * bn
            plain = jnp.logical_and(col >= plain_cols[0], col < plain_cols[1])
        for c in range(bn // HEAD_DIM):
            xs = acc[:, c * HEAD_DIM:(c + 1) * HEAD_DIM]
            y = xs * cos + pltpu.roll(xs, HEAD_DIM // 2, 1) * sin
            if plain_cols is not None:
                y = jnp.where(plain, xs, y)
            o_ref[:, c * HEAD_DIM:(c + 1) * HEAD_DIM] = y.astype(o_ref.dtype)
    else:
        raise ValueError(kind)


def _mm_kernel(a_ref, b_ref, *rest, kind, scale, nk, n_extra, emit_w, w_t, has_prev, plain_cols):
    extra = rest[:n_extra]
    outs = rest[n_extra + (1 if has_prev else 0):]
    o_ref = outs[0]
    w_ref = b_ref
    if emit_w:
        w_ref = outs[1]
        w32 = b_ref[...].T if w_t else b_ref[...]
        w_ref[...] = w32.astype(BF16)
    if kind != "residual":
        acc = jnp.dot(a_ref[...], w_ref[...], preferred_element_type=F32)
        _epilogue(acc, kind, extra, o_ref, scale, plain_cols)
        return
    if nk > 1:
        @pl.when(pl.program_id(2) == 0)
        def _():
            o_ref[...] = extra[0][...]

    bn = o_ref.shape[1]
    cw = min(bn, RES_DOT_CHUNK)
    for c in range(bn // cw):
        cols = slice(c * cw, (c + 1) * cw)
        part = jnp.dot(a_ref[...], w_ref[:, cols], preferred_element_type=F32)
        if nk == 1:
            o_ref[:, cols] = extra[0][:, cols] + part
        else:
            o_ref[:, cols] += part


def _mm_call(a, b, y_prev, *, row0, nrows, col0, n, bm, bn, bk, out_dtype, kind, res, rope, scale,
             emit_w, layer=0, w_t=False):
    m, kdim = a.shape
    nk = kdim // bk
    jb0 = col0 // bn
    assert n % bn == 0 and kdim % bk == 0 and col0 % bn == 0
    assert nk == 1 or (kind == "residual" and out_dtype == F32)
    if w_t:
        assert emit_w and b.ndim == 3
        b_spec = pl.BlockSpec((None, bn, bk), lambda i, j, k: (layer, j + jb0, k))
    elif b.ndim == 3:
        b_spec = pl.BlockSpec((None, bk, bn), lambda i, j, k: (layer, k, j + jb0))
    else:
        b_spec = pl.BlockSpec((bk, bn), lambda i, j, k: (k, j + jb0))
    in_specs = [pl.BlockSpec((bm, bk), lambda i, j, k: (i + row0, k)), b_spec]
    args = [a, b]
    if kind == "residual":
        in_specs.append(pl.BlockSpec((bm, bn), lambda i, j, k: (i + row0, j)))
        args.append(res)
    plain_cols = None
    if kind == "rope":
        cos, sin, seq, plain_cols = rope
        nb = seq // bm
        assert seq % bm == 0
        assert plain_cols is None or (plain_cols[0] % bn == 0 and plain_cols[1] % bn == 0)
        for t in (cos, sin):
            in_specs.append(pl.BlockSpec((bm, HEAD_DIM), lambda i, j, k: ((i + row0) % nb, 0)))
            args.append(t)
    n_extra = len(args) - 2
    aliases = {}
    if y_prev is not None:
        aliases = {len(args): 0}
        in_specs.append(pl.BlockSpec(memory_space=pl.ANY))
        args.append(y_prev)
    out_specs = [pl.BlockSpec((bm, bn), lambda i, j, k: (i + row0, j))]
    out_shape = [jax.ShapeDtypeStruct((m, n), out_dtype)]
    if emit_w:
        out_specs.append(pl.BlockSpec((bk, bn), lambda i, j, k: (k, j)))
        out_shape.append(jax.ShapeDtypeStruct((kdim, n), BF16))
    return pl.pallas_call(
        functools.partial(_mm_kernel, kind=kind, scale=scale, nk=nk, n_extra=n_extra, emit_w=emit_w,
                          w_t=w_t, has_prev=y_prev is not None, plain_cols=plain_cols),
        grid=(nrows, n // bn, nk),
        in_specs=in_specs,
        out_specs=out_specs,
        out_shape=out_shape,
        input_output_aliases=aliases,
        compiler_params=_cparams(("parallel", "parallel", "arbitrary")),
        name="mm_" + kind + ("_w" if emit_w else ""),
    )(*args)


def matmul(a, w32, *, bm, bn, bk=None, out_dtype, kind="cast", res=None, rope=None, scale=None,
           col0=0, n=None, layer=0, bn_first=512, bk_first=None, w_t=False):
    m, kdim = a.shape
    n = w32.shape[-2 if w_t else -1] if n is None else n
    bk = kdim if bk is None else min(bk, kdim)
    bk_first = bk if bk_first is None else min(bk_first, kdim)
    bm = min(bm, m)
    assert m % bm == 0 and w32.shape[-1 if w_t else -2] == kdim
    common = dict(n=n, bm=bm, out_dtype=out_dtype, kind=kind, res=res, rope=rope, scale=scale)
    align = math.gcd(n, *(rope[3] if rope is not None and rope[3] is not None else ()))
    y, wb = _mm_call(a, w32, None, row0=0, nrows=1, col0=col0, bn=math.gcd(bn_first, align, col0),
                     bk=bk_first, emit_w=True, layer=layer, w_t=w_t, **common)
    if m > bm:
        y = _mm_call(a, wb, y, row0=1, nrows=m // bm - 1, col0=0, bn=math.gcd(bn, align), bk=bk,
                     emit_w=False, **common)[0]
    return y


def _softmax_pv(t, v, scale):
    m = jnp.max(t, axis=-1, keepdims=True)
    p = jnp.exp2((t - m) * (scale * LOG2E))
    l = jnp.sum(p, axis=-1, keepdims=True)
    o = jnp.dot(p.astype(BF16), v, preferred_element_type=F32)
    return o / l


def _indexer_kernel(qi_lo_ref, qi_hi_ref, ki_ref, wi_ref, o_ref, *, n_idx_heads, q0, topk):
    tq, s_adm = o_ref.shape[1], o_ref.shape[2]
    q_chunk = (q0 + lax.broadcasted_iota(jnp.int32, (tq, s_adm), 0)) // CHUNK
    k_chunk = lax.broadcasted_iota(jnp.int32, (tq, s_adm), 1) // CHUNK
    admissible = k_chunk <= q_chunk
    if s_adm <= topk:
        o_ref[0] = jnp.where(admissible, 0.0, NEG_BIG).astype(o_ref.dtype)
        return

    ki = ki_ref[0]
    wi = wi_ref[...]
    score = jnp.zeros((tq, s_adm), F32)
    half = n_idx_heads // 2
    for h in range(n_idx_heads):
        qi_ref, hh = (qi_lo_ref, h) if h < half else (qi_hi_ref, h - half)
        d = lax.dot_general(qi_ref[:, hh * IDX_DIM:(hh + 1) * IDX_DIM], ki, NT_DIMS,
                            preferred_element_type=F32)
        score = score + wi[:, h:h + 1] * jnp.maximum(d, 0.0)

    bits = pltpu.bitcast(score, jnp.int32)
    key = jnp.where(bits < 0, bits ^ jnp.int32(0x7FFFFFFF), bits)
    key = jnp.where(admissible, key, jnp.int32(INT_MIN))

    kf = jnp.float32(topk)
    cnt = jnp.sum(jnp.where(key >= 0, 1.0, 0.0), axis=-1, keepdims=True)
    lo0 = jnp.where(cnt >= kf, jnp.int32(0), jnp.int32(INT_MIN))

    def body(i, lo):
        cand = lo + (jnp.int32(1) << (30 - i))
        c = jnp.sum(jnp.where(key >= cand, 1.0, 0.0), axis=-1, keepdims=True)
        return jnp.where(c >= kf, cand, lo)

    thr = lax.fori_loop(0, 31, body, lo0)
    sel = jnp.logical_and(admissible, key >= thr)
    o_ref[0] = jnp.where(sel, 0.0, NEG_BIG).astype(o_ref.dtype)

    n_sel = jnp.sum(jnp.where(sel, 1.0, 0.0), axis=-1, keepdims=True)

    @pl.when(jnp.max(n_sel) > kf)
    def _():
        gt = key > thr
        eq = jnp.logical_and(admissible, key == thr)
        need = kf - jnp.sum(jnp.where(gt, 1.0, 0.0), axis=-1, keepdims=True)
        idx = lax.broadcasted_iota(jnp.int32, (tq, s_adm), 1)
        nbits = max(1, (s_adm - 1).bit_length())

        def tie_body(i, last):
            cand = last + (jnp.int32(1) << (nbits - 1 - i))
            c = jnp.sum(jnp.where(jnp.logical_and(eq, idx <= cand), 1.0, 0.0), axis=-1, keepdims=True)
            return jnp.where(c < need, cand, last)

        last = lax.fori_loop(0, nbits, tie_body, jnp.full((tq, 1), -1, jnp.int32))
        keep = jnp.logical_or(gt, jnp.logical_and(eq, idx <= last + 1))
        o_ref[0] = jnp.where(keep, 0.0, NEG_BIG).astype(o_ref.dtype)


def indexer_mask(proj, ki3, wi, *, qi_col0, n_idx_heads, bsz, seq, qb, tq, topk):
    s_adm = (qb + 1) * tq
    nqb = seq // tq
    half_w = n_idx_heads // 2 * IDX_DIM
    assert n_idx_heads % 2 == 0 and qi_col0 % half_w == 0
    cb = qi_col0 // half_w
    return pl.pallas_call(
        functools.partial(_indexer_kernel, n_idx_heads=n_idx_heads, q0=qb * tq, topk=topk),
        grid=(bsz,),
        in_specs=[pl.BlockSpec((tq, half_w), lambda b: (b * nqb + qb, cb)),
                  pl.BlockSpec((tq, half_w), lambda b: (b * nqb + qb, cb + 1)),
                  pl.BlockSpec((1, s_adm, IDX_DIM), lambda b: (b, 0, 0)),
                  pl.BlockSpec((tq, LANES), lambda b: (b * nqb + qb, 0))],
        out_specs=pl.BlockSpec((1, tq, s_adm), lambda b: (b, 0, 0)),
        out_shape=jax.ShapeDtypeStruct((bsz, tq, s_adm), F32),
        compiler_params=_cparams(("parallel",)),
        name="dsa_indexer",
    )(proj, proj, ki3, wi)


DSA_HEADS_PER_STEP = 4


def _dsa_attn_kernel(q_ref, k_ref, v_ref, *rest, tq):
    mask_refs, o_ref = rest[:-1], rest[-1]
    for qb, m_ref in enumerate(mask_refs):
        s_adm = (qb + 1) * tq
        rows = slice(qb * tq, (qb + 1) * tq)
        k = k_ref[0, :s_adm, :]
        v = v_ref[0, :s_adm, :]
        mask = m_ref[0]
        for j in range(DSA_HEADS_PER_STEP):
            cols = slice(j * HEAD_DIM, (j + 1) * HEAD_DIM)
            t = lax.dot_general(q_ref[rows, cols], k, NT_DIMS, preferred_element_type=F32) + mask
            o_ref[rows, cols] = _softmax_pv(t, v, HEAD_DIM ** -0.5).astype(o_ref.dtype)


def dsa_attention(proj, masks, *, n_heads, bsz, seq, tq):
    m = proj.shape[0]
    n_kv = n_heads // KV_GROUP
    proj3 = proj.reshape(bsz, seq, proj.shape[1])
    hw = DSA_HEADS_PER_STEP * HEAD_DIM
    per_group = KV_GROUP // DSA_HEADS_PER_STEP
    q_blk = lambda b, g, j: (b, g * per_group + j)
    in_specs = [pl.BlockSpec((seq, hw), q_blk),
                pl.BlockSpec((1, seq, HEAD_DIM), lambda b, g, j: (b, 0, n_heads + g)),
                pl.BlockSpec((1, seq, HEAD_DIM), lambda b, g, j: (b, 0, n_heads + n_kv + g))]
    in_specs += [pl.BlockSpec((1, tq, mk.shape[2]), lambda b, g, j: (b, 0, 0),
                              pipeline_mode=pl.Buffered(1)) for mk in masks]
    return pl.pallas_call(
        functools.partial(_dsa_attn_kernel, tq=tq),
        grid=(bsz, n_kv, per_group),
        in_specs=in_specs,
        out_specs=pl.BlockSpec((seq, hw), q_blk),
        out_shape=jax.ShapeDtypeStruct((m, n_heads * HEAD_DIM), BF16),
        compiler_params=_cparams(("parallel", "parallel", "parallel")),
        name="dsa_attn",
    )(proj, proj3, proj3, *masks)


BAND_HEADS_PER_STEP = 4


def _band_attn_kernel(q_ref, k_ref, v_ref, g_ref, o_ref, bias_ref, *, tq, pad):
    seq = q_ref.shape[1]
    width = tq + pad
    scale = HEAD_DIM ** -0.5

    @pl.when(pl.program_id(1) == 0)
    def _():
        lp = g_ref.shape[2]
        tc = lax.broadcasted_iota(jnp.int32, (tq, width), 0) // CHUNK
        jc = lax.broadcasted_iota(jnp.int32, (tq, width), 1) // CHUNK
        in_band = jnp.logical_and(jc >= tc, jc <= tc + LEFT_CHUNKS)
        for hh in range(BAND_HEADS_PER_STEP):
            rows = jnp.broadcast_to(g_ref[hh], (tq, lp))
            toe = pltpu.roll(rows, lp - tq + 1, 1, stride=1, stride_axis=0)[:, :width]
            bias_ref[hh] = jnp.where(in_band, toe * (1.0 / scale), NEG_BIG)

    for qb in range(seq // tq):
        q0 = qb * tq
        klo = max(0, q0 - pad)
        nk = q0 + tq - klo
        for hh in range(BAND_HEADS_PER_STEP):
            cols = slice(hh * HEAD_DIM, (hh + 1) * HEAD_DIM)
            t = lax.dot_general(q_ref[0, q0:q0 + tq, cols], k_ref[0, klo:klo + nk, cols], NT_DIMS,
                                preferred_element_type=F32) + bias_ref[hh, :, width - nk:]
            o_ref[0, q0:q0 + tq, cols] = _softmax_pv(t, v_ref[0, klo:klo + nk, cols],
                                                     scale).astype(o_ref.dtype)


def band_attention(qkv3, rel_rows, *, tq):
    bsz, seq, d3 = qkv3.shape
    n_heads = d3 // (3 * HEAD_DIM)
    pad = LEFT_CHUNKS * CHUNK
    hps = BAND_HEADS_PER_STEP
    assert n_heads % hps == 0
    steps = n_heads // hps
    w = hps * HEAD_DIM
    return pl.pallas_call(
        functools.partial(_band_attn_kernel, tq=tq, pad=pad),
        grid=(steps, bsz),
        in_specs=[pl.BlockSpec((1, seq, w), lambda h, b: (b, 0, h)),
                  pl.BlockSpec((1, seq, w), lambda h, b: (b, 0, steps + h)),
                  pl.BlockSpec((1, seq, w), lambda h, b: (b, 0, 2 * steps + h)),
                  pl.BlockSpec((hps, 1, rel_rows.shape[2]), lambda h, b: (h, 0, 0))],
        out_specs=pl.BlockSpec((1, seq, w), lambda h, b: (b, 0, h)),
        out_shape=jax.ShapeDtypeStruct((bsz, seq, n_heads * HEAD_DIM), BF16),
        scratch_shapes=[pltpu.VMEM((hps, tq, tq + pad), F32)],
        compiler_params=_cparams(("parallel", "arbitrary")),
        name="band_attn",
    )(qkv3, qkv3, qkv3, rel_rows)


def band_rel_rows(rel_bias, tq):
    pad = LEFT_CHUNKS * CHUNK
    lp = -(-(2 * tq + pad - 1) // LANES) * LANES
    rel = tq + pad - 1 - jnp.arange(lp)
    idx = jnp.clip(rel, -REL_CLIP, REL_CLIP) + REL_CLIP
    return rel_bias[:, idx].astype(F32)[:, None, :]


def _mem_block_kernel(x_ref, g1_ref, wq_ref, k_ref, v_ref, wo_ref, g2_ref, x_out_ref, h_out_ref):
    x = x_ref[...]
    h = _rms(x, g1_ref[...]).astype(BF16)
    q = jnp.dot(h, wq_ref[...], preferred_element_type=F32).astype(BF16)
    scale = MEM_HEAD_DIM ** -0.5
    outs = []
    for hd in range(MEM_HEADS):
        sl = slice(hd * MEM_HEAD_DIM, (hd + 1) * MEM_HEAD_DIM)
        t = lax.dot_general(q[:, sl], k_ref[0, :, sl], NT_DIMS, preferred_element_type=F32)
        outs.append(_softmax_pv(t, v_ref[0, :, sl], scale).astype(BF16))
    o = jnp.concatenate(outs, axis=-1)
    y = x + jnp.dot(o, wo_ref[...], preferred_element_type=F32)
    x_out_ref[...] = y
    h_out_ref[...] = _rms(y, g2_ref[...]).astype(h_out_ref.dtype)


def mem_block(x, g1, wq, km3, vm3, wo, g2, *, seq, tm=512):
    m, d = x.shape
    md = wq.shape[1]
    n_mem = km3.shape[1]
    per_b = seq // tm
    const2 = lambda i: (0, 0)
    once = pl.Buffered(1)
    return pl.pallas_call(
        _mem_block_kernel,
        grid=(m // tm,),
        in_specs=[pl.BlockSpec((tm, d), lambda i: (i, 0)),
                  pl.BlockSpec((1, d), const2, pipeline_mode=once),
                  pl.BlockSpec((d, md), const2, pipeline_mode=once),
                  pl.BlockSpec((1, n_mem, md), lambda i: (i // per_b, 0, 0)),
                  pl.BlockSpec((1, n_mem, md), lambda i: (i // per_b, 0, 0)),
                  pl.BlockSpec((md, d), const2, pipeline_mode=once),
                  pl.BlockSpec((1, d), const2, pipeline_mode=once)],
        out_specs=[pl.BlockSpec((tm, d), lambda i: (i, 0)),
                   pl.BlockSpec((tm, d), lambda i: (i, 0))],
        out_shape=[jax.ShapeDtypeStruct((m, d), F32),
                   jax.ShapeDtypeStruct((m, d), BF16)],
        compiler_params=_cparams(("parallel",)),
        name="mem_block",
    )(x, g1.reshape(1, d), wq, km3, vm3, wo, g2.reshape(1, d))


def _rope_tables(seq):
    half = HEAD_DIM // 2
    inv = ROPE_THETA ** (-jnp.arange(half, dtype=F32) / half)
    ang = jnp.arange(seq, dtype=F32)[:, None] * inv[None, :]
    cos, sin = jnp.cos(ang), jnp.sin(ang)
    return jnp.concatenate([cos, cos], axis=-1), jnp.concatenate([-sin, sin], axis=-1)


def _pad_cols(w, n):
    return jnp.pad(w, ((0, 0), (0, n - w.shape[1])))


def kernel(x, mem, g_mix, g_mem_attn, g_ffn, g_memory, g_final, w_in_a, w_out_a, w_in_b, rel_bias,
           w_out_b, w_mq, w_mk, w_mv, w_mo, w_up, w_down):
    bsz, seq, d = x.shape
    depth = g_mix.shape[0]
    m = bsz * seq
    n_heads = d // HEAD_DIM
    n_kv = n_heads // KV_GROUP
    a_q, a_kv, a_iq = n_heads * HEAD_DIM, n_kv * HEAD_DIM, n_heads * IDX_DIM
    n_idx_heads = n_heads
    topk = min(TOPK_MAX, seq // 4)
    tq = 256
    bm = 1024

    xf = x.reshape(m, d)
    cos, sin = _rope_tables(seq)

    n_mem = mem.shape[1]
    mem_n = rmsnorm(mem.reshape(bsz * n_mem, d), g_memory, BF16)

    for i in range(depth):
        j = i // 2
        h = rmsnorm(xf, g_mix[i], BF16)
        if i % 2 == 0:
            w = w_in_a
            c0, c1, c2, c3, c4 = a_q, a_q + a_kv, a_q + 2 * a_kv, a_q + 2 * a_kv + a_iq, a_q + 2 * a_kv + a_iq + IDX_DIM
            mm = functools.partial(matmul, h, jnp.swapaxes(w, 1, 2), w_t=True, layer=j, bm=bm, bn=1024,
                                   out_dtype=BF16, kind="rope")
            proj = mm(col0=0, n=c3, rope=(cos, sin, seq, (c1, c2)))
            ki = mm(col0=c3, n=IDX_DIM, rope=(cos, sin, seq, None))
            wi = matmul(h, _pad_cols(w[j, :, c4:], LANES), bm=bm, bn=LANES, out_dtype=F32, kind="scale",
                        scale=(n_idx_heads ** -0.5) * (IDX_DIM ** -0.5))
            ki3 = ki.reshape(bsz, seq, IDX_DIM)
            masks = [indexer_mask(proj, ki3, wi, qi_col0=c2, n_idx_heads=n_idx_heads, bsz=bsz, seq=seq,
                                  qb=qb, tq=tq, topk=topk) for qb in range(seq // tq)]
            o = dsa_attention(proj, masks, n_heads=n_heads, bsz=bsz, seq=seq, tq=tq)
            xf = matmul(o, w_out_a, layer=j, bm=bm, bn=1024, out_dtype=F32, kind="residual", res=xf)
        else:
            qkv = matmul(h, w_in_b, layer=j, bm=bm, bn=1024, out_dtype=BF16)
            o = band_attention(qkv.reshape(bsz, seq, 3 * d), band_rel_rows(rel_bias[j], tq), tq=tq)
            xf = matmul(o.reshape(m, d), w_out_b, layer=j, bm=bm, bn=1024, out_dtype=F32,
                        kind="residual", res=xf)

        md = w_mq.shape[2]
        km = matmul(mem_n, w_mk, layer=i, bm=bm, bn=md, out_dtype=BF16).reshape(bsz, n_mem, md)
        vm = matmul(mem_n, w_mv, layer=i, bm=bm, bn=md, out_dtype=BF16).reshape(bsz, n_mem, md)
        xf, h2 = mem_block(xf, g_mem_attn[i], w_mq[i].astype(BF16), km, vm, w_mo[i].astype(BF16),
                           g_ffn[i], seq=seq)

        u = matmul(h2, w_up, layer=i, bm=bm, bn=1024, out_dtype=BF16, kind="relu2")
        xf = matmul(u, w_down, layer=i, bm=bm, bn=1024, bk=4096, bn_first=1024, bk_first=2048,
                    out_dtype=F32, kind="residual", res=xf)

    return rmsnorm(xf, g_final, F32).reshape(bsz, seq, d)
```

```python
import functools
import math

import jax
import jax.numpy as jnp
from jax import lax
from jax.experimental import pallas as pl
from jax.experimental.pallas import tpu as pltpu

CHUNK = 64
HEAD_DIM = 128
KV_GROUP = 4
IDX_DIM = 128
TOPK_MAX = 256
LEFT_CHUNKS = 8
REL_CLIP = 256
MEM_HEADS = 4
MEM_HEAD_DIM = 128
ROPE_THETA = 10000.0
EPS = 1e-6

LANES = 128
V7X_VMEM_BYTES = 64 * 1024 * 1024
VMEM_LIMIT = V7X_VMEM_BYTES - 8 * 1024 * 1024

RES_DOT_CHUNK = 512

LOG2E = 1.4426950408889634
NEG_BIG = -1e30
INT_MIN = -(2 ** 31)

F32 = jnp.float32
BF16 = jnp.bfloat16
NT_DIMS = (((1,), (1,)), ((), ()))


def _cparams(sem):
    return pltpu.CompilerParams(dimension_semantics=sem, vmem_limit_bytes=VMEM_LIMIT)


def _rms(x, g):
    return x * lax.rsqrt(jnp.mean(x * x, axis=-1, keepdims=True) + EPS) * g


def _rmsnorm_kernel(x_ref, g_ref, o_ref):
    o_ref[...] = _rms(x_ref[...], g_ref[...]).astype(o_ref.dtype)


def rmsnorm(x, g, out_dtype, tm=512):
    m, d = x.shape
    tm = min(tm, m)
    return pl.pallas_call(
        _rmsnorm_kernel,
        grid=(m // tm,),
        in_specs=[pl.BlockSpec((tm, d), lambda i: (i, 0)),
                  pl.BlockSpec((1, d), lambda i: (0, 0))],
        out_specs=pl.BlockSpec((tm, d), lambda i: (i, 0)),
        out_shape=jax.ShapeDtypeStruct((m, d), out_dtype),
        compiler_params=_cparams(("parallel",)),
        name="rmsnorm",
    )(x, g.reshape(1, d))


def _epilogue(acc, kind, extra_refs, o_ref, scale, plain_cols=None):
    if kind == "cast":
        o_ref[...] = acc.astype(o_ref.dtype)
    elif kind == "scale":
        o_ref[...] = (acc * scale).astype(o_ref.dtype)
    elif kind == "relu2":
        a = jnp.maximum(acc, 0.0)
        o_ref[...] = (a * a).astype(o_ref.dtype)
    elif kind == "residual":
        o_ref[...] = (extra_refs[0][...] + acc).astype(o_ref.dtype)
    elif kind == "rope":
        cos = extra_refs[0][...]
        sin = extra_refs[1][...]
        bn = acc.shape[1]
        if plain_cols is not None:
            col = pl.program_id(1) * bn
            plain = jnp.logical_and(col >= plain_cols[0], col < plain_cols[1])
        for c in range(bn // HEAD_DIM):
            xs = acc[:, c * HEAD_DIM:(c + 1) * HEAD_DIM]
            y = xs * cos + pltpu.roll(xs, HEAD_DIM // 2, 1) * sin
            if plain_cols is not None:
                y = jnp.where(plain, xs, y)
            o_ref[:, c * HEAD_DIM:(c + 1) * HEAD_DIM] = y.astype(o_ref.dtype)
    else:
        raise ValueError(kind)


def _mm_kernel(a_ref, b_ref, *rest, kind, scale, nk, n_extra, emit_w, w_t, has_prev, plain_cols):
    extra = rest[:n_extra]
    outs = rest[n_extra + (1 if has_prev else 0):]
    o_ref = outs[0]
    w_ref = b_ref
    if emit_w:
        w_ref = outs[1]
        w32 = b_ref[...].T if w_t else b_ref[...]
        w_ref[...] = w32.astype(BF16)
    if kind != "residual":
        acc = jnp.dot(a_ref[...], w_ref[...], preferred_element_type=F32)
        _epilogue(acc, kind, extra, o_ref, scale, plain_cols)
        return
    if nk > 1:
        @pl.when(pl.program_id(2) == 0)
        def _():
            o_ref[...] = extra[0][...]

    bn = o_ref.shape[1]
    cw = min(bn, RES_DOT_CHUNK)
    for c in range(bn // cw):
        cols = slice(c * cw, (c + 1) * cw)
        part = jnp.dot(a_ref[...], w_ref[:, cols], preferred_element_type=F32)
        if nk == 1:
            o_ref[:, cols] = extra[0][:, cols] + part
        else:
            o_ref[:, cols] += part


def _mm_call(a, b, y_prev, *, row0, nrows, col0, n, bm, bn, bk, out_dtype, kind, res, rope, scale,
             emit_w, layer=0, w_t=False):
    m, kdim = a.shape
    nk = kdim // bk
    jb0 = col0 // bn
    assert n % bn == 0 and kdim % bk == 0 and col0 % bn == 0
    assert nk == 1 or (kind == "residual" and out_dtype == F32)
    if w_t:
        assert emit_w and b.ndim == 3
        b_spec = pl.BlockSpec((None, bn, bk), lambda i, j, k: (layer, j + jb0, k))
    elif b.ndim == 3:
        b_spec = pl.BlockSpec((None, bk, bn), lambda i, j, k: (layer, k, j + jb0))
    else:
        b_spec = pl.BlockSpec((bk, bn), lambda i, j, k: (k, j + jb0))
    in_specs = [pl.BlockSpec((bm, bk), lambda i, j, k: (i + row0, k)), b_spec]
    args = [a, b]
    if kind == "residual":
        in_specs.append(pl.BlockSpec((bm, bn), lambda i, j, k: (i + row0, j)))
        args.append(res)
    plain_cols = None
    if kind == "rope":
        cos, sin, seq, plain_cols = rope
        nb = seq // bm
        assert seq % bm == 0
        assert plain_cols is None or (plain_cols[0] % bn == 0 and plain_cols[1] % bn == 0)
        for t in (cos, sin):
            in_specs.append(pl.BlockSpec((bm, HEAD_DIM), lambda i, j, k: ((i + row0) % nb, 0)))
            args.append(t)
    n_extra = len(args) - 2
    aliases = {}
    if y_prev is not None:
        aliases = {len(args): 0}
        in_specs.append(pl.BlockSpec(memory_space=pl.ANY))
        args.append(y_prev)
    out_specs = [pl.BlockSpec((bm, bn), lambda i, j, k: (i + row0, j))]
    out_shape = [jax.ShapeDtypeStruct((m, n), out_dtype)]
    if emit_w:
        out_specs.append(pl.BlockSpec((bk, bn), lambda i, j, k: (k, j)))
        out_shape.append(jax.ShapeDtypeStruct((kdim, n), BF16))
    return pl.pallas_call(
        functools.partial(_mm_kernel, kind=kind, scale=scale, nk=nk, n_extra=n_extra, emit_w=emit_w,
                          w_t=w_t, has_prev=y_prev is not None, plain_cols=plain_cols),
        grid=(nrows, n // bn, nk),
        in_specs=in_specs,
        out_specs=out_specs,
        out_shape=out_shape,
        input_output_aliases=aliases,
        compiler_params=_cparams(("parallel", "parallel", "arbitrary")),
        name="mm_" + kind + ("_w" if emit_w else ""),
    )(*args)


def matmul(a, w32, *, bm, bn, bk=None, out_dtype, kind="cast", res=None, rope=None, scale=None,
           col0=0, n=None, layer=0, bn_first=512, bk_first=None, w_t=False):
    m, kdim = a.shape
    n = w32.shape[-2 if w_t else -1] if n is None else n
    bk = kdim if bk is None else min(bk, kdim)
    bk_first = bk if bk_first is None else min(bk_first, kdim)
    bm = min(bm, m)
    assert m % bm == 0 and w32.shape[-1 if w_t else -2] == kdim
    common = dict(n=n, bm=bm, out_dtype=out_dtype, kind=kind, res=res, rope=rope, scale=scale)
    align = math.gcd(n, *(rope[3] if rope is not None and rope[3] is not None else ()))
    y, wb = _mm_call(a, w32, None, row0=0, nrows=1, col0=col0, bn=math.gcd(bn_first, align, col0),
                     bk=bk_first, emit_w=True, layer=layer, w_t=w_t, **common)
    if m > bm:
        y = _mm_call(a, wb, y, row0=1, nrows=m // bm - 1, col0=0, bn=math.gcd(bn, align), bk=bk,
                     emit_w=False, **common)[0]
    return y


def _exp_scores(t, scale):
    m = jnp.max(t, axis=-1, keepdims=True)
    return jnp.exp2((t - m) * (scale * LOG2E)).astype(BF16)


def _with_ones(v):
    return jnp.concatenate([v, jnp.ones_like(v)], axis=-1)


def _normalised(o):
    hd = o.shape[1] // 2
    return o[:, :hd] / o[:, hd:hd + 1]


def _softmax_pv(t, v, scale):
    return _normalised(jnp.dot(_exp_scores(t, scale), _with_ones(v), preferred_element_type=F32))


def _indexer_kernel(qi_lo_ref, qi_hi_ref, ki_ref, wi_ref, o_ref, *, n_idx_heads, q0, topk):
    tq, s_adm = o_ref.shape[1], o_ref.shape[2]
    q_chunk = (q0 + lax.broadcasted_iota(jnp.int32, (tq, s_adm), 0)) // CHUNK
    k_chunk = lax.broadcasted_iota(jnp.int32, (tq, s_adm), 1) // CHUNK
    admissible = k_chunk <= q_chunk
    if s_adm <= topk:
        o_ref[0] = jnp.where(admissible, 0.0, NEG_BIG).astype(o_ref.dtype)
        return

    ki = ki_ref[0]
    wi = wi_ref[...]
    score = jnp.zeros((tq, s_adm), F32)
    half = n_idx_heads // 2
    for h in range(n_idx_heads):
        qi_ref, hh = (qi_lo_ref, h) if h < half else (qi_hi_ref, h - half)
        d = lax.dot_general(qi_ref[:, hh * IDX_DIM:(hh + 1) * IDX_DIM], ki, NT_DIMS,
                            preferred_element_type=F32)
        score = score + wi[:, h:h + 1] * jnp.maximum(d, 0.0)

    bits = pltpu.bitcast(score, jnp.int32)
    key = jnp.where(bits < 0, bits ^ jnp.int32(0x7FFFFFFF), bits)
    key = jnp.where(admissible, key, jnp.int32(INT_MIN))

    kf = jnp.float32(topk)
    cnt = jnp.sum(jnp.where(key >= 0, 1.0, 0.0), axis=-1, keepdims=True)
    lo0 = jnp.where(cnt >= kf, jnp.int32(0), jnp.int32(INT_MIN))

    def body(i, lo):
        cand = lo + (jnp.int32(1) << (30 - i))
        c = jnp.sum(jnp.where(key >= cand, 1.0, 0.0), axis=-1, keepdims=True)
        return jnp.where(c >= kf, cand, lo)

    thr = lax.fori_loop(0, 31, body, lo0)
    sel = jnp.logical_and(admissible, key >= thr)
    o_ref[0] = jnp.where(sel, 0.0, NEG_BIG).astype(o_ref.dtype)

    n_sel = jnp.sum(jnp.where(sel, 1.0, 0.0), axis=-1, keepdims=True)

    @pl.when(jnp.max(n_sel) > kf)
    def _():
        gt = key > thr
        eq = jnp.logical_and(admissible, key == thr)
        need = kf - jnp.sum(jnp.where(gt, 1.0, 0.0), axis=-1, keepdims=True)
        idx = lax.broadcasted_iota(jnp.int32, (tq, s_adm), 1)
        nbits = max(1, (s_adm - 1).bit_length())

        def tie_body(i, last):
            cand = last + (jnp.int32(1) << (nbits - 1 - i))
            c = jnp.sum(jnp.where(jnp.logical_and(eq, idx <= cand), 1.0, 0.0), axis=-1, keepdims=True)
            return jnp.where(c < need, cand, last)

        last = lax.fori_loop(0, nbits, tie_body, jnp.full((tq, 1), -1, jnp.int32))
        keep = jnp.logical_or(gt, jnp.logical_and(eq, idx <= last + 1))
        o_ref[0] = jnp.where(keep, 0.0, NEG_BIG).astype(o_ref.dtype)


def indexer_mask(proj, ki3, wi, *, qi_col0, n_idx_heads, bsz, seq, qb, tq, topk):
    s_adm = (qb + 1) * tq
    nqb = seq // tq
    half_w = n_idx_heads // 2 * IDX_DIM
    assert n_idx_heads % 2 == 0 and qi_col0 % half_w == 0
    cb = qi_col0 // half_w
    return pl.pallas_call(
        functools.partial(_indexer_kernel, n_idx_heads=n_idx_heads, q0=qb * tq, topk=topk),
        grid=(bsz,),
        in_specs=[pl.BlockSpec((tq, half_w), lambda b: (b * nqb + qb, cb)),
                  pl.BlockSpec((tq, half_w), lambda b: (b * nqb + qb, cb + 1)),
                  pl.BlockSpec((1, s_adm, IDX_DIM), lambda b: (b, 0, 0)),
                  pl.BlockSpec((tq, LANES), lambda b: (b * nqb + qb, 0))],
        out_specs=pl.BlockSpec((1, tq, s_adm), lambda b: (b, 0, 0)),
        out_shape=jax.ShapeDtypeStruct((bsz, tq, s_adm), F32),
        compiler_params=_cparams(("parallel",)),
        name="dsa_indexer",
    )(proj, proj, ki3, wi)


DSA_HEADS_PER_STEP = 4


def _dsa_attn_kernel(q_ref, k_ref, v_ref, *rest, tq):
    mask_refs, o_ref = rest[:-1], rest[-1]
    scale = HEAD_DIM ** -0.5
    for qb, m_ref in enumerate(mask_refs):
        s_adm = (qb + 1) * tq
        rows = slice(qb * tq, (qb + 1) * tq)
        k = k_ref[0, :s_adm, :]
        v1 = _with_ones(v_ref[0, :s_adm, :])
        mask = m_ref[0]
        ps = []
        for j in range(DSA_HEADS_PER_STEP):
            cols = slice(j * HEAD_DIM, (j + 1) * HEAD_DIM)
            t = lax.dot_general(q_ref[rows, cols], k, NT_DIMS, preferred_element_type=F32) + mask
            ps.append(_exp_scores(t, scale))
        o = jnp.dot(jnp.concatenate(ps, axis=0), v1, preferred_element_type=F32)
        for j in range(DSA_HEADS_PER_STEP):
            cols = slice(j * HEAD_DIM, (j + 1) * HEAD_DIM)
            o_ref[rows, cols] = _normalised(o[j * tq:(j + 1) * tq]).astype(o_ref.dtype)


def dsa_attention(proj, masks, *, n_heads, bsz, seq, tq):
    m = proj.shape[0]
    n_kv = n_heads // KV_GROUP
    proj3 = proj.reshape(bsz, seq, proj.shape[1])
    hw = DSA_HEADS_PER_STEP * HEAD_DIM
    per_group = KV_GROUP // DSA_HEADS_PER_STEP
    q_blk = lambda b, g, j: (b, g * per_group + j)
    in_specs = [pl.BlockSpec((seq, hw), q_blk),
                pl.BlockSpec((1, seq, HEAD_DIM), lambda b, g, j: (b, 0, n_heads + g)),
                pl.BlockSpec((1, seq, HEAD_DIM), lambda b, g, j: (b, 0, n_heads + n_kv + g))]
    in_specs += [pl.BlockSpec((1, tq, mk.shape[2]), lambda b, g, j: (b, 0, 0),
                              pipeline_mode=pl.Buffered(1)) for mk in masks]
    return pl.pallas_call(
        functools.partial(_dsa_attn_kernel, tq=tq),
        grid=(bsz, n_kv, per_group),
        in_specs=in_specs,
        out_specs=pl.BlockSpec((seq, hw), q_blk),
        out_shape=jax.ShapeDtypeStruct((m, n_heads * HEAD_DIM), BF16),
        compiler_params=_cparams(("parallel", "parallel", "parallel")),
        name="dsa_attn",
    )(proj, proj3, proj3, *masks)


BAND_HEADS_PER_STEP = 4


def _band_attn_kernel(q_ref, k_ref, v_ref, g_ref, o_ref, bias_ref, *, tq, pad):
    seq = q_ref.shape[1]
    width = tq + pad
    scale = HEAD_DIM ** -0.5

    @pl.when(pl.program_id(1) == 0)
    def _():
        lp = g_ref.shape[2]
        tc = lax.broadcasted_iota(jnp.int32, (tq, width), 0) // CHUNK
        jc = lax.broadcasted_iota(jnp.int32, (tq, width), 1) // CHUNK
        in_band = jnp.logical_and(jc >= tc, jc <= tc + LEFT_CHUNKS)
        for hh in range(BAND_HEADS_PER_STEP):
            rows = jnp.broadcast_to(g_ref[hh], (tq, lp))
            toe = pltpu.roll(rows, lp - tq + 1, 1, stride=1, stride_axis=0)[:, :width]
            bias_ref[hh] = jnp.where(in_band, toe * (1.0 / scale), NEG_BIG)

    for qb in range(seq // tq):
        q0 = qb * tq
        klo = max(0, q0 - pad)
        nk = q0 + tq - klo
        for hh in range(BAND_HEADS_PER_STEP):
            cols = slice(hh * HEAD_DIM, (hh + 1) * HEAD_DIM)
            t = lax.dot_general(q_ref[0, q0:q0 + tq, cols], k_ref[0, klo:klo + nk, cols], NT_DIMS,
                                preferred_element_type=F32) + bias_ref[hh, :, width - nk:]
            o_ref[0, q0:q0 + tq, cols] = _softmax_pv(t, v_ref[0, klo:klo + nk, cols],
                                                     scale).astype(o_ref.dtype)


def band_attention(qkv3, rel_rows, *, tq):
    bsz, seq, d3 = qkv3.shape
    n_heads = d3 // (3 * HEAD_DIM)
    pad = LEFT_CHUNKS * CHUNK
    hps = BAND_HEADS_PER_STEP
    assert n_heads % hps == 0
    steps = n_heads // hps
    w = hps * HEAD_DIM
    return pl.pallas_call(
        functools.partial(_band_attn_kernel, tq=tq, pad=pad),
        grid=(steps, bsz),
        in_specs=[pl.BlockSpec((1, seq, w), lambda h, b: (b, 0, h)),
                  pl.BlockSpec((1, seq, w), lambda h, b: (b, 0, steps + h)),
                  pl.BlockSpec((1, seq, w), lambda h, b: (b, 0, 2 * steps + h)),
                  pl.BlockSpec((hps, 1, rel_rows.shape[2]), lambda h, b: (h, 0, 0))],
        out_specs=pl.BlockSpec((1, seq, w), lambda h, b: (b, 0, h)),
        out_shape=jax.ShapeDtypeStruct((bsz, seq, n_heads * HEAD_DIM), BF16),
        scratch_shapes=[pltpu.VMEM((hps, tq, tq + pad), F32)],
        compiler_params=_cparams(("parallel", "arbitrary")),
        name="band_attn",
    )(qkv3, qkv3, qkv3, rel_rows)


def band_rel_rows(rel_bias, tq):
    pad = LEFT_CHUNKS * CHUNK
    lp = -(-(2 * tq + pad - 1) // LANES) * LANES
    rel = tq + pad - 1 - jnp.arange(lp)
    idx = jnp.clip(rel, -REL_CLIP, REL_CLIP) + REL_CLIP
    return rel_bias[:, idx].astype(F32)[:, None, :]


def _mem_block_kernel(x_ref, g1_ref, wq_ref, k_ref, v_ref, wo_ref, g2_ref, x_out_ref, h_out_ref):
    x = x_ref[...]
    h = _rms(x, g1_ref[...]).astype(BF16)
    q = jnp.dot(h, wq_ref[...], preferred_element_type=F32).astype(BF16)
    scale = MEM_HEAD_DIM ** -0.5
    outs = []
    for hd in range(MEM_HEADS):
        sl = slice(hd * MEM_HEAD_DIM, (hd + 1) * MEM_HEAD_DIM)
        t = lax.dot_general(q[:, sl], k_ref[0, :, sl], NT_DIMS, preferred_element_type=F32)
        outs.append(_softmax_pv(t, v_ref[0, :, sl], scale).astype(BF16))
    o = jnp.concatenate(outs, axis=-1)
    y = x + jnp.dot(o, wo_ref[...], preferred_element_type=F32)
    x_out_ref[...] = y
    h_out_ref[...] = _rms(y, g2_ref[...]).astype(h_out_ref.dtype)


def mem_block(x, g1, wq, km3, vm3, wo, g2, *, seq, tm=512):
    m, d = x.shape
    md = wq.shape[1]
    n_mem = km3.shape[1]
    per_b = seq // tm
    const2 = lambda i: (0, 0)
    once = pl.Buffered(1)
    return pl.pallas_call(
        _mem_block_kernel,
        grid=(m // tm,),
        in_specs=[pl.BlockSpec((tm, d), lambda i: (i, 0)),
                  pl.BlockSpec((1, d), const2, pipeline_mode=once),
                  pl.BlockSpec((d, md), const2, pipeline_mode=once),
                  pl.BlockSpec((1, n_mem, md), lambda i: (i // per_b, 0, 0)),
                  pl.BlockSpec((1, n_mem, md), lambda i: (i // per_b, 0, 0)),
                  pl.BlockSpec((md, d), const2, pipeline_mode=once),
                  pl.BlockSpec((1, d), const2, pipeline_mode=once)],
        out_specs=[pl.BlockSpec((tm, d), lambda i: (i, 0)),
                   pl.BlockSpec((tm, d), lambda i: (i, 0))],
        out_shape=[jax.ShapeDtypeStruct((m, d), F32),
                   jax.ShapeDtypeStruct((m, d), BF16)],
        compiler_params=_cparams(("parallel",)),
        name="mem_block",
    )(x, g1.reshape(1, d), wq, km3, vm3, wo, g2.reshape(1, d))


def _rope_tables(seq):
    half = HEAD_DIM // 2
    inv = ROPE_THETA ** (-jnp.arange(half, dtype=F32) / half)
    ang = jnp.arange(seq, dtype=F32)[:, None] * inv[None, :]
    cos, sin = jnp.cos(ang), jnp.sin(ang)
    return jnp.concatenate([cos, cos], axis=-1), jnp.concatenate([-sin, sin], axis=-1)


def _pad_cols(w, n):
    return jnp.pad(w, ((0, 0), (0, n - w.shape[1])))


def kernel(x, mem, g_mix, g_mem_attn, g_ffn, g_memory, g_final, w_in_a, w_out_a, w_in_b, rel_bias,
           w_out_b, w_mq, w_mk, w_mv, w_mo, w_up, w_down):
    bsz, seq, d = x.shape
    depth = g_mix.shape[0]
    m = bsz * seq
    n_heads = d // HEAD_DIM
    n_kv = n_heads // KV_GROUP
    a_q, a_kv, a_iq = n_heads * HEAD_DIM, n_kv * HEAD_DIM, n_heads * IDX_DIM
    n_idx_heads = n_heads
    topk = min(TOPK_MAX, seq // 4)
    tq = 256
    bm = 1024

    xf = x.reshape(m, d)
    cos, sin = _rope_tables(seq)

    n_mem = mem.shape[1]
    mem_n = rmsnorm(mem.reshape(bsz * n_mem, d), g_memory, BF16)

    for i in range(depth):
        j = i // 2
        h = rmsnorm(xf, g_mix[i], BF16)
        if i % 2 == 0:
            w = w_in_a
            c0, c1, c2, c3, c4 = a_q, a_q + a_kv, a_q + 2 * a_kv, a_q + 2 * a_kv + a_iq, a_q + 2 * a_kv + a_iq + IDX_DIM
            mm = functools.partial(matmul, h, jnp.swapaxes(w, 1, 2), w_t=True, layer=j, bm=bm, bn=1024,
                                   out_dtype=BF16, kind="rope")
            proj = mm(col0=0, n=c3, rope=(cos, sin, seq, (c1, c2)))
            ki = mm(col0=c3, n=IDX_DIM, rope=(cos, sin, seq, None))
            wi = matmul(h, _pad_cols(w[j, :, c4:], LANES), bm=bm, bn=LANES, out_dtype=F32, kind="scale",
                        scale=(n_idx_heads ** -0.5) * (IDX_DIM ** -0.5))
            ki3 = ki.reshape(bsz, seq, IDX_DIM)
            masks = [indexer_mask(proj, ki3, wi, qi_col0=c2, n_idx_heads=n_idx_heads, bsz=bsz, seq=seq,
                                  qb=qb, tq=tq, topk=topk) for qb in range(seq // tq)]
            o = dsa_attention(proj, masks, n_heads=n_heads, bsz=bsz, seq=seq, tq=tq)
            xf = matmul(o, w_out_a, layer=j, bm=bm, bn=1024, out_dtype=F32, kind="residual", res=xf)
        else:
            qkv = matmul(h, w_in_b, layer=j, bm=bm, bn=1024, out_dtype=BF16)
            o = band_attention(qkv.reshape(bsz, seq, 3 * d), band_rel_rows(rel_bias[j], tq), tq=tq)
            xf = matmul(o.reshape(m, d), w_out_b, layer=j, bm=bm, bn=1024, out_dtype=F32,
                        kind="residual", res=xf)

        md = w_mq.shape[2]
        km = matmul(mem_n, w_mk, layer=i, bm=bm, bn=md, out_dtype=BF16).reshape(bsz, n_mem, md)
        vm = matmul(mem_n, w_mv, layer=i, bm=bm, bn=md, out_dtype=BF16).reshape(bsz, n_mem, md)
        xf, h2 = mem_block(xf, g_mem_attn[i], w_mq[i].astype(BF16), km, vm, w_mo[i].astype(BF16),
                           g_ffn[i], seq=seq)

        u = matmul(h2, w_up, layer=i, bm=bm, bn=1024, out_dtype=BF16, kind="relu2")
        xf = matmul(u, w_down, layer=i, bm=bm, bn=1024, bk=4096, bn_first=1024, bk_first=2048,
                    out_dtype=F32, kind="residual", res=xf)

    return rmsnorm(xf, g_final, F32).reshape(bsz, seq, d)
```

```python
import functools
import math

import jax
import jax.numpy as jnp
from jax import lax
from jax.experimental import pallas as pl
from jax.experimental.pallas import tpu as pltpu

CHUNK = 64
HEAD_DIM = 128
KV_GROUP = 4
IDX_DIM = 128
TOPK_MAX = 256
LEFT_CHUNKS = 8
REL_CLIP = 256
MEM_HEADS = 4
MEM_HEAD_DIM = 128
ROPE_THETA = 10000.0
EPS = 1e-6

LANES = 128
V7X_VMEM_BYTES = 64 * 1024 * 1024
VMEM_LIMIT = V7X_VMEM_BYTES - 8 * 1024 * 1024

MM_BLOCK_M = 1024
MM_BLOCK_N = 1024
MM_BLOCK_K = 4096
MM_FIRST_BLOCK_N = 512
RES_DOT_CHUNK = 512
ATTN_Q_BLOCK = 256

LOG2E = 1.4426950408889634
NEG_BIG = -1e30
INT_MIN = -(2 ** 31)

F32 = jnp.float32
BF16 = jnp.bfloat16
NT_DIMS = (((1,), (1,)), ((), ()))


def _cparams(sem):
    return pltpu.CompilerParams(dimension_semantics=sem, vmem_limit_bytes=VMEM_LIMIT)


def _rms(x, g):
    return x * lax.rsqrt(jnp.mean(x * x, axis=-1, keepdims=True) + EPS) * g


def _rmsnorm_kernel(x_ref, g_ref, o_ref):
    o_ref[...] = _rms(x_ref[...], g_ref[...]).astype(o_ref.dtype)


def rmsnorm(x, g, out_dtype, tm=512):
    m, d = x.shape
    tm = min(tm, m)
    return pl.pallas_call(
        _rmsnorm_kernel,
        grid=(m // tm,),
        in_specs=[pl.BlockSpec((tm, d), lambda i: (i, 0)),
                  pl.BlockSpec((1, d), lambda i: (0, 0))],
        out_specs=pl.BlockSpec((tm, d), lambda i: (i, 0)),
        out_shape=jax.ShapeDtypeStruct((m, d), out_dtype),
        compiler_params=_cparams(("parallel",)),
        name="rmsnorm",
    )(x, g.reshape(1, d))


def _epilogue(acc, kind, extra_refs, o_ref, plain_cols):
    if kind == "cast":
        o_ref[...] = acc.astype(o_ref.dtype)
    elif kind == "relu2":
        a = jnp.maximum(acc, 0.0)
        o_ref[...] = (a * a).astype(o_ref.dtype)
    elif kind == "rope":
        cos = extra_refs[0][...]
        sin = extra_refs[1][...]
        bn = acc.shape[1]
        for c in range(bn // HEAD_DIM):
            xs = acc[:, c * HEAD_DIM:(c + 1) * HEAD_DIM]
            y = xs * cos + pltpu.roll(xs, HEAD_DIM // 2, 1) * sin
            if plain_cols is not None:
                col = pl.program_id(1) * bn + c * HEAD_DIM
                y = jnp.where(jnp.logical_and(col >= plain_cols[0], col < plain_cols[1]), xs, y)
            o_ref[:, c * HEAD_DIM:(c + 1) * HEAD_DIM] = y.astype(o_ref.dtype)
    else:
        raise ValueError(kind)


def _mm_kernel(a_ref, b_ref, *rest, kind, nk, n_extra, emit_w, w_t, has_prev, plain_cols):
    extra = rest[:n_extra]
    outs = rest[n_extra + (1 if has_prev else 0):]
    o_ref = outs[0]
    w_ref = b_ref
    if emit_w:
        w_ref = outs[1]
        w32 = b_ref[...].T if w_t else b_ref[...]
        w_ref[...] = w32.astype(BF16)
    if kind != "residual":
        acc = jnp.dot(a_ref[...], w_ref[...], preferred_element_type=F32)
        _epilogue(acc, kind, extra, o_ref, plain_cols)
        return
    if nk > 1:
        @pl.when(pl.program_id(2) == 0)
        def _():
            o_ref[...] = extra[0][...]

    bn = o_ref.shape[1]
    cw = min(bn, RES_DOT_CHUNK)
    for c in range(bn // cw):
        cols = slice(c * cw, (c + 1) * cw)
        part = jnp.dot(a_ref[...], w_ref[:, cols], preferred_element_type=F32)
        if nk == 1:
            o_ref[:, cols] = extra[0][:, cols] + part
        else:
            o_ref[:, cols] += part


def _mm_call(a, b, y_prev, *, row0, nrows, col0, n, bm, bn, bk, out_dtype, kind, res, rope,
             emit_w, layer=0, w_t=False):
    m, kdim = a.shape
    nk = kdim // bk
    jb0 = col0 // bn
    assert n % bn == 0 and kdim % bk == 0 and col0 % bn == 0
    assert nk == 1 or (kind == "residual" and out_dtype == F32)
    if w_t:
        assert emit_w and b.ndim == 3
        b_spec = pl.BlockSpec((None, bn, bk), lambda i, j, k: (layer, j + jb0, k))
    elif b.ndim == 3:
        b_spec = pl.BlockSpec((None, bk, bn), lambda i, j, k: (layer, k, j + jb0))
    else:
        b_spec = pl.BlockSpec((bk, bn), lambda i, j, k: (k, j + jb0))
    in_specs = [pl.BlockSpec((bm, bk), lambda i, j, k: (i + row0, k)), b_spec]
    args = [a, b]
    if kind == "residual":
        in_specs.append(pl.BlockSpec((bm, bn), lambda i, j, k: (i + row0, j)))
        args.append(res)
    plain_cols = None
    if kind == "rope":
        cos, sin, seq, plain_cols = rope
        nb = seq // bm
        assert seq % bm == 0
        assert plain_cols is None or all(c % HEAD_DIM == 0 for c in plain_cols)
        for t in (cos, sin):
            in_specs.append(pl.BlockSpec((bm, HEAD_DIM), lambda i, j, k: ((i + row0) % nb, 0)))
            args.append(t)
    n_extra = len(args) - 2
    aliases = {}
    if y_prev is not None:
        aliases = {len(args): 0}
        in_specs.append(pl.BlockSpec(memory_space=pl.ANY))
        args.append(y_prev)
    out_specs = [pl.BlockSpec((bm, bn), lambda i, j, k: (i + row0, j))]
    out_shape = [jax.ShapeDtypeStruct((m, n), out_dtype)]
    if emit_w:
        out_specs.append(pl.BlockSpec((bk, bn), lambda i, j, k: (k, j)))
        out_shape.append(jax.ShapeDtypeStruct((kdim, n), BF16))
    return pl.pallas_call(
        functools.partial(_mm_kernel, kind=kind, nk=nk, n_extra=n_extra, emit_w=emit_w,
                          w_t=w_t, has_prev=y_prev is not None, plain_cols=plain_cols),
        grid=(nrows, n // bn, nk),
        in_specs=in_specs,
        out_specs=out_specs,
        out_shape=out_shape,
        input_output_aliases=aliases,
        compiler_params=_cparams(("parallel", "parallel", "arbitrary")),
        name="mm_" + kind + ("_w" if emit_w else ""),
    )(*args)


def matmul(a, w32, *, bm, bn, bk=None, out_dtype, kind="cast", res=None, rope=None, col0=0, n=None,
           layer=0, bn_first=MM_FIRST_BLOCK_N, bk_first=None, w_t=False):
    m, kdim = a.shape
    n = w32.shape[-2 if w_t else -1] if n is None else n
    bk = kdim if bk is None else min(bk, kdim)
    bk_first = bk if bk_first is None else min(bk_first, kdim)
    bm = min(bm, m)
    assert m % bm == 0 and w32.shape[-1 if w_t else -2] == kdim
    common = dict(n=n, bm=bm, out_dtype=out_dtype, kind=kind, res=res, rope=rope)
    y, wb = _mm_call(a, w32, None, row0=0, nrows=1, col0=col0, bn=math.gcd(bn_first, n, col0),
                     bk=bk_first, emit_w=True, layer=layer, w_t=w_t, **common)
    if m > bm:
        y = _mm_call(a, wb, y, row0=1, nrows=m // bm - 1, col0=0, bn=math.gcd(bn, n), bk=bk,
                     emit_w=False, **common)[0]
    return y


def _exp_scores(t, scale):
    m = jnp.max(t, axis=-1, keepdims=True)
    return jnp.exp2((t - m) * (scale * LOG2E)).astype(BF16)


def _with_ones(v):
    return jnp.concatenate([v, jnp.ones_like(v)], axis=-1)


def _normalised(o):
    hd = o.shape[1] // 2
    return o[:, :hd] / o[:, hd:hd + 1]


def _softmax_pv(t, v, scale):
    return _normalised(jnp.dot(_exp_scores(t, scale), _with_ones(v), preferred_element_type=F32))


def _indexer_kernel(qi_lo_ref, qi_hi_ref, ki_ref, wi_ref, o_ref, *, n_idx_heads, q0, topk, wi_scale):
    tq, s_adm = o_ref.shape[1], o_ref.shape[2]
    q_chunk = (q0 + lax.broadcasted_iota(jnp.int32, (tq, s_adm), 0)) // CHUNK
    k_chunk = lax.broadcasted_iota(jnp.int32, (tq, s_adm), 1) // CHUNK
    admissible = k_chunk <= q_chunk
    if s_adm <= topk:
        o_ref[0] = jnp.where(admissible, 0.0, NEG_BIG).astype(o_ref.dtype)
        return

    ki = ki_ref[0].astype(BF16)
    wi = wi_ref[...] * wi_scale
    score = jnp.zeros((tq, s_adm), F32)
    half = n_idx_heads // 2
    for h in range(n_idx_heads):
        qi_ref, hh = (qi_lo_ref, h) if h < half else (qi_hi_ref, h - half)
        d = lax.dot_general(qi_ref[:, hh * IDX_DIM:(hh + 1) * IDX_DIM], ki, NT_DIMS,
                            preferred_element_type=F32)
        score = score + wi[:, h:h + 1] * jnp.maximum(d, 0.0)

    bits = pltpu.bitcast(score, jnp.int32)
    key = jnp.where(bits < 0, bits ^ jnp.int32(0x7FFFFFFF), bits)
    key = jnp.where(admissible, key, jnp.int32(INT_MIN))

    kf = jnp.float32(topk)
    cnt = jnp.sum(jnp.where(key >= 0, 1.0, 0.0), axis=-1, keepdims=True)
    lo0 = jnp.where(cnt >= kf, jnp.int32(0), jnp.int32(INT_MIN))

    def body(i, lo):
        cand = lo + (jnp.int32(1) << (30 - i))
        c = jnp.sum(jnp.where(key >= cand, 1.0, 0.0), axis=-1, keepdims=True)
        return jnp.where(c >= kf, cand, lo)

    thr = lax.fori_loop(0, 31, body, lo0)
    sel = jnp.logical_and(admissible, key >= thr)
    o_ref[0] = jnp.where(sel, 0.0, NEG_BIG).astype(o_ref.dtype)

    n_sel = jnp.sum(jnp.where(sel, 1.0, 0.0), axis=-1, keepdims=True)

    @pl.when(jnp.max(n_sel) > kf)
    def _():
        gt = key > thr
        eq = jnp.logical_and(admissible, key == thr)
        need = kf - jnp.sum(jnp.where(gt, 1.0, 0.0), axis=-1, keepdims=True)
        idx = lax.broadcasted_iota(jnp.int32, (tq, s_adm), 1)
        nbits = max(1, (s_adm - 1).bit_length())

        def tie_body(i, last):
            cand = last + (jnp.int32(1) << (nbits - 1 - i))
            c = jnp.sum(jnp.where(jnp.logical_and(eq, idx <= cand), 1.0, 0.0), axis=-1, keepdims=True)
            return jnp.where(c < need, cand, last)

        last = lax.fori_loop(0, nbits, tie_body, jnp.full((tq, 1), -1, jnp.int32))
        keep = jnp.logical_or(gt, jnp.logical_and(eq, idx <= last + 1))
        o_ref[0] = jnp.where(keep, 0.0, NEG_BIG).astype(o_ref.dtype)


def indexer_mask(proj, kw, *, qi_col0, n_idx_heads, wi_scale, bsz, seq, qb, tq, topk):
    s_adm = (qb + 1) * tq
    nqb = seq // tq
    half_w = n_idx_heads // 2 * IDX_DIM
    assert n_idx_heads % 2 == 0 and qi_col0 % half_w == 0 and IDX_DIM == LANES
    cb = qi_col0 // half_w
    kw3 = kw.reshape(bsz, seq, kw.shape[1])
    return pl.pallas_call(
        functools.partial(_indexer_kernel, n_idx_heads=n_idx_heads, q0=qb * tq, topk=topk,
                          wi_scale=wi_scale),
        grid=(bsz,),
        in_specs=[pl.BlockSpec((tq, half_w), lambda b: (b * nqb + qb, cb)),
                  pl.BlockSpec((tq, half_w), lambda b: (b * nqb + qb, cb + 1)),
                  pl.BlockSpec((1, s_adm, IDX_DIM), lambda b: (b, 0, 0)),
                  pl.BlockSpec((tq, LANES), lambda b: (b * nqb + qb, 1))],
        out_specs=pl.BlockSpec((1, tq, s_adm), lambda b: (b, 0, 0)),
        out_shape=jax.ShapeDtypeStruct((bsz, tq, s_adm), F32),
        compiler_params=_cparams(("parallel",)),
        name="dsa_indexer",
    )(proj, proj, kw3, kw)


DSA_HEADS_PER_STEP = 4


def _dsa_attn_kernel(q_ref, k_ref, v_ref, *rest, tq):
    mask_refs, o_ref = rest[:-1], rest[-1]
    scale = HEAD_DIM ** -0.5
    for qb, m_ref in enumerate(mask_refs):
        s_adm = (qb + 1) * tq
        rows = slice(qb * tq, (qb + 1) * tq)
        k = k_ref[0, :s_adm, :]
        v1 = _with_ones(v_ref[0, :s_adm, :])
        mask = m_ref[0]
        ps = []
        for j in range(DSA_HEADS_PER_STEP):
            cols = slice(j * HEAD_DIM, (j + 1) * HEAD_DIM)
            t = lax.dot_general(q_ref[rows, cols], k, NT_DIMS, preferred_element_type=F32) + mask
            ps.append(_exp_scores(t, scale))
        o = jnp.dot(jnp.concatenate(ps, axis=0), v1, preferred_element_type=F32)
        for j in range(DSA_HEADS_PER_STEP):
            cols = slice(j * HEAD_DIM, (j + 1) * HEAD_DIM)
            o_ref[rows, cols] = _normalised(o[j * tq:(j + 1) * tq]).astype(o_ref.dtype)


def dsa_attention(proj, masks, *, n_heads, bsz, seq, tq):
    m = proj.shape[0]
    n_kv = n_heads // KV_GROUP
    proj3 = proj.reshape(bsz, seq, proj.shape[1])
    hw = DSA_HEADS_PER_STEP * HEAD_DIM
    per_group = KV_GROUP // DSA_HEADS_PER_STEP
    q_blk = lambda b, g, j: (b, g * per_group + j)
    in_specs = [pl.BlockSpec((seq, hw), q_blk),
                pl.BlockSpec((1, seq, HEAD_DIM), lambda b, g, j: (b, 0, n_heads + g)),
                pl.BlockSpec((1, seq, HEAD_DIM), lambda b, g, j: (b, 0, n_heads + n_kv + g))]
    in_specs += [pl.BlockSpec((1, tq, mk.shape[2]), lambda b, g, j: (b, 0, 0),
                              pipeline_mode=pl.Buffered(1)) for mk in masks]
    return pl.pallas_call(
        functools.partial(_dsa_attn_kernel, tq=tq),
        grid=(bsz, n_kv, per_group),
        in_specs=in_specs,
        out_specs=pl.BlockSpec((seq, hw), q_blk),
        out_shape=jax.ShapeDtypeStruct((m, n_heads * HEAD_DIM), BF16),
        compiler_params=_cparams(("parallel", "parallel", "parallel")),
        name="dsa_attn",
    )(proj, proj3, proj3, *masks)


BAND_HEADS_PER_STEP = 4


def _band_attn_kernel(q_ref, k_ref, v_ref, g_ref, o_ref, bias_ref, *, tq, pad):
    seq = q_ref.shape[1]
    width = tq + pad
    scale = HEAD_DIM ** -0.5

    @pl.when(pl.program_id(1) == 0)
    def _():
        lp = g_ref.shape[2]
        tc = lax.broadcasted_iota(jnp.int32, (tq, width), 0) // CHUNK
        jc = lax.broadcasted_iota(jnp.int32, (tq, width), 1) // CHUNK
        in_band = jnp.logical_and(jc >= tc, jc <= tc + LEFT_CHUNKS)
        for hh in range(BAND_HEADS_PER_STEP):
            rows = jnp.broadcast_to(g_ref[hh], (tq, lp))
            toe = pltpu.roll(rows, lp - tq + 1, 1, stride=1, stride_axis=0)[:, :width]
            bias_ref[hh] = jnp.where(in_band, toe * (1.0 / scale), NEG_BIG)

    for qb in range(seq // tq):
        q0 = qb * tq
        klo = max(0, q0 - pad)
        nk = q0 + tq - klo
        for hh in range(BAND_HEADS_PER_STEP):
            cols = slice(hh * HEAD_DIM, (hh + 1) * HEAD_DIM)
            t = lax.dot_general(q_ref[0, q0:q0 + tq, cols], k_ref[0, klo:klo + nk, cols], NT_DIMS,
                                preferred_element_type=F32) + bias_ref[hh, :, width - nk:]
            o_ref[0, q0:q0 + tq, cols] = _softmax_pv(t, v_ref[0, klo:klo + nk, cols],
                                                     scale).astype(o_ref.dtype)


def band_attention(qkv3, rel_rows, *, tq):
    bsz, seq, d3 = qkv3.shape
    n_heads = d3 // (3 * HEAD_DIM)
    pad = LEFT_CHUNKS * CHUNK
    hps = BAND_HEADS_PER_STEP
    assert n_heads % hps == 0
    steps = n_heads // hps
    w = hps * HEAD_DIM
    return pl.pallas_call(
        functools.partial(_band_attn_kernel, tq=tq, pad=pad),
        grid=(steps, bsz),
        in_specs=[pl.BlockSpec((1, seq, w), lambda h, b: (b, 0, h)),
                  pl.BlockSpec((1, seq, w), lambda h, b: (b, 0, steps + h)),
                  pl.BlockSpec((1, seq, w), lambda h, b: (b, 0, 2 * steps + h)),
                  pl.BlockSpec((hps, 1, rel_rows.shape[2]), lambda h, b: (h, 0, 0))],
        out_specs=pl.BlockSpec((1, seq, w), lambda h, b: (b, 0, h)),
        out_shape=jax.ShapeDtypeStruct((bsz, seq, n_heads * HEAD_DIM), BF16),
        scratch_shapes=[pltpu.VMEM((hps, tq, tq + pad), F32)],
        compiler_params=_cparams(("parallel", "arbitrary")),
        name="band_attn",
    )(qkv3, qkv3, qkv3, rel_rows)


def band_rel_rows(rel_bias, tq):
    pad = LEFT_CHUNKS * CHUNK
    lp = -(-(2 * tq + pad - 1) // LANES) * LANES
    rel = tq + pad - 1 - jnp.arange(lp)
    idx = jnp.clip(rel, -REL_CLIP, REL_CLIP) + REL_CLIP
    return rel_bias[:, idx].astype(F32)[:, None, :]


def _mem_block_kernel(x_ref, g1_ref, wq_ref, k_ref, v_ref, wo_ref, g2_ref, x_out_ref, h_out_ref):
    x = x_ref[...]
    h = _rms(x, g1_ref[...]).astype(BF16)
    q = jnp.dot(h, wq_ref[...], preferred_element_type=F32).astype(BF16)
    scale = MEM_HEAD_DIM ** -0.5
    outs = []
    for hd in range(MEM_HEADS):
        sl = slice(hd * MEM_HEAD_DIM, (hd + 1) * MEM_HEAD_DIM)
        t = lax.dot_general(q[:, sl], k_ref[0, :, sl], NT_DIMS, preferred_element_type=F32)
        outs.append(_softmax_pv(t, v_ref[0, :, sl], scale).astype(BF16))
    o = jnp.concatenate(outs, axis=-1)
    y = x + jnp.dot(o, wo_ref[...], preferred_element_type=F32)
    x_out_ref[...] = y
    h_out_ref[...] = _rms(y, g2_ref[...]).astype(h_out_ref.dtype)


def mem_block(x, g1, wq, km3, vm3, wo, g2, *, seq, tm=512):
    m, d = x.shape
    md = wq.shape[1]
    n_mem = km3.shape[1]
    per_b = seq // tm
    const2 = lambda i: (0, 0)
    once = pl.Buffered(1)
    return pl.pallas_call(
        _mem_block_kernel,
        grid=(m // tm,),
        in_specs=[pl.BlockSpec((tm, d), lambda i: (i, 0)),
                  pl.BlockSpec((1, d), const2, pipeline_mode=once),
                  pl.BlockSpec((d, md), const2, pipeline_mode=once),
                  pl.BlockSpec((1, n_mem, md), lambda i: (i // per_b, 0, 0)),
                  pl.BlockSpec((1, n_mem, md), lambda i: (i // per_b, 0, 0)),
                  pl.BlockSpec((md, d), const2, pipeline_mode=once),
                  pl.BlockSpec((1, d), const2, pipeline_mode=once)],
        out_specs=[pl.BlockSpec((tm, d), lambda i: (i, 0)),
                   pl.BlockSpec((tm, d), lambda i: (i, 0))],
        out_shape=[jax.ShapeDtypeStruct((m, d), F32),
                   jax.ShapeDtypeStruct((m, d), BF16)],
        compiler_params=_cparams(("parallel",)),
        name="mem_block",
    )(x, g1.reshape(1, d), wq, km3, vm3, wo, g2.reshape(1, d))


def _rope_tables(seq):
    half = HEAD_DIM // 2
    inv = ROPE_THETA ** (-jnp.arange(half, dtype=F32) / half)
    ang = jnp.arange(seq, dtype=F32)[:, None] * inv[None, :]
    cos, sin = jnp.cos(ang), jnp.sin(ang)
    return jnp.concatenate([cos, cos], axis=-1), jnp.concatenate([-sin, sin], axis=-1)


def kernel(x, mem, g_mix, g_mem_attn, g_ffn, g_memory, g_final, w_in_a, w_out_a, w_in_b, rel_bias,
           w_out_b, w_mq, w_mk, w_mv, w_mo, w_up, w_down):
    bsz, seq, d = x.shape
    depth = g_mix.shape[0]
    m = bsz * seq
    n_heads = d // HEAD_DIM
    n_kv = n_heads // KV_GROUP
    a_q, a_kv, a_iq = n_heads * HEAD_DIM, n_kv * HEAD_DIM, n_heads * IDX_DIM
    n_idx_heads = n_heads
    topk = min(TOPK_MAX, seq // 4)
    tq = ATTN_Q_BLOCK
    bm, bn = MM_BLOCK_M, MM_BLOCK_N

    xf = x.reshape(m, d)
    cos, sin = _rope_tables(seq)

    n_mem = mem.shape[1]
    mem_n = rmsnorm(mem.reshape(bsz * n_mem, d), g_memory, BF16)

    for i in range(depth):
        j = i // 2
        h = rmsnorm(xf, g_mix[i], BF16)
        if i % 2 == 0:
            c1, c2, c3 = a_q + a_kv, a_q + 2 * a_kv, a_q + 2 * a_kv + a_iq
            wt = jnp.swapaxes(w_in_a, 1, 2)
            proj = matmul(h, wt, w_t=True, layer=j, n=c3, bm=bm, bn=bn, out_dtype=BF16, kind="rope",
                          rope=(cos, sin, seq, (c1, c2)))
            w_kw = jnp.pad(wt[j:j + 1, c3:], ((0, 0), (0, 2 * LANES - (wt.shape[1] - c3)), (0, 0)))
            kw = matmul(h, w_kw, w_t=True, bm=bm, bn=2 * LANES, out_dtype=F32, kind="rope",
                        rope=(cos, sin, seq, (IDX_DIM, 2 * LANES)))
            wi_scale = (n_idx_heads ** -0.5) * (IDX_DIM ** -0.5)
            masks = [indexer_mask(proj, kw, qi_col0=c2, n_idx_heads=n_idx_heads, wi_scale=wi_scale, bsz=bsz,
                                  seq=seq, qb=qb, tq=tq, topk=topk) for qb in range(seq // tq)]
            o = dsa_attention(proj, masks, n_heads=n_heads, bsz=bsz, seq=seq, tq=tq)
            xf = matmul(o, w_out_a, layer=j, bm=bm, bn=bn, out_dtype=F32, kind="residual", res=xf)
        else:
            qkv = matmul(h, w_in_b, layer=j, bm=bm, bn=bn, out_dtype=BF16)
            o = band_attention(qkv.reshape(bsz, seq, 3 * d), band_rel_rows(rel_bias[j], tq), tq=tq)
            xf = matmul(o.reshape(m, d), w_out_b, layer=j, bm=bm, bn=bn, out_dtype=F32,
                        kind="residual", res=xf)

        md = w_mq.shape[2]
        km = matmul(mem_n, w_mk, layer=i, bm=bm, bn=md, out_dtype=BF16).reshape(bsz, n_mem, md)
        vm = matmul(mem_n, w_mv, layer=i, bm=bm, bn=md, out_dtype=BF16).reshape(bsz, n_mem, md)
        xf, h2 = mem_block(xf, g_mem_attn[i], w_mq[i].astype(BF16), km, vm, w_mo[i].astype(BF16),
                           g_ffn[i], seq=seq)

        u = matmul(h2, w_up, layer=i, bm=bm, bn=bn, out_dtype=BF16, kind="relu2")
        xf = matmul(u, w_down, layer=i, bm=bm, bn=bn, bk=MM_BLOCK_K, bn_first=bn, bk_first=MM_BLOCK_K // 2,
                    out_dtype=F32, kind="residual", res=xf)

    return rmsnorm(xf, g_final, F32).reshape(bsz, seq, d)
```

```python
import functools
import math

import jax
import jax.numpy as jnp
from jax import lax
from jax.experimental import pallas as pl
from jax.experimental.pallas import tpu as pltpu

CHUNK = 64
HEAD_DIM = 128
KV_GROUP = 4
IDX_DIM = 128
TOPK_MAX = 256
LEFT_CHUNKS = 8
REL_CLIP = 256
MEM_HEADS = 4
MEM_HEAD_DIM = 128
ROPE_THETA = 10000.0
EPS = 1e-6

LANES = 128
V7X_VMEM_BYTES = 64 * 1024 * 1024
VMEM_LIMIT = V7X_VMEM_BYTES - 8 * 1024 * 1024

MM_BLOCK_M = 1024
MM_BLOCK_N = 1024
MM_BLOCK_K = 4096
MM_FIRST_BLOCK_N = 512
RES_DOT_CHUNK = 512
ATTN_Q_BLOCK = 256

LOG2E = 1.4426950408889634
NEG_BIG = -1e30
INT_MIN = -(2 ** 31)

F32 = jnp.float32
BF16 = jnp.bfloat16
NT_DIMS = (((1,), (1,)), ((), ()))


def _cparams(sem):
    return pltpu.CompilerParams(dimension_semantics=sem, vmem_limit_bytes=VMEM_LIMIT)


def _rms(x, g):
    return x * lax.rsqrt(jnp.mean(x * x, axis=-1, keepdims=True) + EPS) * g


def _rmsnorm_kernel(x_ref, g_ref, o_ref):
    o_ref[...] = _rms(x_ref[...], g_ref[...]).astype(o_ref.dtype)


def rmsnorm(x, g, out_dtype, tm=512):
    m, d = x.shape
    tm = min(tm, m)
    return pl.pallas_call(
        _rmsnorm_kernel,
        grid=(m // tm,),
        in_specs=[pl.BlockSpec((tm, d), lambda i: (i, 0)),
                  pl.BlockSpec((1, d), lambda i: (0, 0))],
        out_specs=pl.BlockSpec((tm, d), lambda i: (i, 0)),
        out_shape=jax.ShapeDtypeStruct((m, d), out_dtype),
        compiler_params=_cparams(("parallel",)),
        name="rmsnorm",
    )(x, g.reshape(1, d))


def _epilogue(acc, kind, extra_refs, o_ref, plain_cols):
    if kind == "cast":
        o_ref[...] = acc.astype(o_ref.dtype)
    elif kind == "relu2":
        a = jnp.maximum(acc, 0.0)
        o_ref[...] = (a * a).astype(o_ref.dtype)
    elif kind == "rope":
        cos = extra_refs[0][...]
        sin = extra_refs[1][...]
        bn = acc.shape[1]
        for c in range(bn // HEAD_DIM):
            xs = acc[:, c * HEAD_DIM:(c + 1) * HEAD_DIM]
            y = xs * cos + pltpu.roll(xs, HEAD_DIM // 2, 1) * sin
            if plain_cols is not None:
                col = pl.program_id(1) * bn + c * HEAD_DIM
                y = jnp.where(jnp.logical_and(col >= plain_cols[0], col < plain_cols[1]), xs, y)
            o_ref[:, c * HEAD_DIM:(c + 1) * HEAD_DIM] = y.astype(o_ref.dtype)
    else:
        raise ValueError(kind)


def _mm_kernel(a_ref, b_ref, *rest, kind, nk, n_extra, emit_w, w_t, has_prev, plain_cols):
    extra = rest[:n_extra]
    outs = rest[n_extra + (1 if has_prev else 0):]
    o_ref = outs[0]
    w_ref = b_ref
    if emit_w:
        w_ref = outs[1]
        w32 = b_ref[...].T if w_t else b_ref[...]
        w_ref[...] = w32.astype(BF16)
    if kind != "residual":
        acc = jnp.dot(a_ref[...], w_ref[...], preferred_element_type=F32)
        _epilogue(acc, kind, extra, o_ref, plain_cols)
        return
    if nk > 1:
        @pl.when(pl.program_id(2) == 0)
        def _():
            o_ref[...] = extra[0][...]

    bn = o_ref.shape[1]
    cw = min(bn, RES_DOT_CHUNK)
    for c in range(bn // cw):
        cols = slice(c * cw, (c + 1) * cw)
        part = jnp.dot(a_ref[...], w_ref[:, cols], preferred_element_type=F32)
        if nk == 1:
            o_ref[:, cols] = extra[0][:, cols] + part
        else:
            o_ref[:, cols] += part


def _mm_call(a, b, y_prev, *, row0, nrows, col0, n, bm, bn, bk, out_dtype, kind, res, rope,
             emit_w, layer=0, w_t=False):
    m, kdim = a.shape
    nk = kdim // bk
    jb0 = col0 // bn
    assert n % bn == 0 and kdim % bk == 0 and col0 % bn == 0
    assert nk == 1 or (kind == "residual" and out_dtype == F32)
    if w_t:
        assert emit_w and b.ndim == 3
        b_spec = pl.BlockSpec((None, bn, bk), lambda i, j, k: (layer, j + jb0, k))
    elif b.ndim == 3:
        b_spec = pl.BlockSpec((None, bk, bn), lambda i, j, k: (layer, k, j + jb0))
    else:
        b_spec = pl.BlockSpec((bk, bn), lambda i, j, k: (k, j + jb0))
    in_specs = [pl.BlockSpec((bm, bk), lambda i, j, k: (i + row0, k)), b_spec]
    args = [a, b]
    if kind == "residual":
        in_specs.append(pl.BlockSpec((bm, bn), lambda i, j, k: (i + row0, j)))
        args.append(res)
    plain_cols = None
    if kind == "rope":
        cos, sin, seq, plain_cols = rope
        nb = seq // bm
        assert seq % bm == 0
        assert plain_cols is None or all(c % HEAD_DIM == 0 for c in plain_cols)
        for t in (cos, sin):
            in_specs.append(pl.BlockSpec((bm, HEAD_DIM), lambda i, j, k: ((i + row0) % nb, 0)))
            args.append(t)
    n_extra = len(args) - 2
    aliases = {}
    if y_prev is not None:
        aliases = {len(args): 0}
        in_specs.append(pl.BlockSpec(memory_space=pl.ANY))
        args.append(y_prev)
    out_specs = [pl.BlockSpec((bm, bn), lambda i, j, k: (i + row0, j))]
    out_shape = [jax.ShapeDtypeStruct((m, n), out_dtype)]
    if emit_w:
        out_specs.append(pl.BlockSpec((bk, bn), lambda i, j, k: (k, j)))
        out_shape.append(jax.ShapeDtypeStruct((kdim, n), BF16))
    return pl.pallas_call(
        functools.partial(_mm_kernel, kind=kind, nk=nk, n_extra=n_extra, emit_w=emit_w,
                          w_t=w_t, has_prev=y_prev is not None, plain_cols=plain_cols),
        grid=(nrows, n // bn, nk),
        in_specs=in_specs,
        out_specs=out_specs,
        out_shape=out_shape,
        input_output_aliases=aliases,
        compiler_params=_cparams(("parallel", "parallel", "arbitrary")),
        name="mm_" + kind + ("_w" if emit_w else ""),
    )(*args)


def matmul(a, w32, *, bm, bn, bk=None, out_dtype, kind="cast", res=None, rope=None, col0=0, n=None,
           layer=0, bn_first=MM_FIRST_BLOCK_N, bk_first=None, w_t=False):
    m, kdim = a.shape
    n = w32.shape[-2 if w_t else -1] if n is None else n
    bk = kdim if bk is None else min(bk, kdim)
    bk_first = bk if bk_first is None else min(bk_first, kdim)
    bm = min(bm, m)
    assert m % bm == 0 and w32.shape[-1 if w_t else -2] == kdim
    common = dict(n=n, bm=bm, out_dtype=out_dtype, kind=kind, res=res, rope=rope)
    y, wb = _mm_call(a, w32, None, row0=0, nrows=1, col0=col0, bn=math.gcd(bn_first, n, col0),
                     bk=bk_first, emit_w=True, layer=layer, w_t=w_t, **common)
    if m > bm:
        y = _mm_call(a, wb, y, row0=1, nrows=m // bm - 1, col0=0, bn=math.gcd(bn, n), bk=bk,
                     emit_w=False, **common)[0]
    return y


def _exp_scores(t, scale):
    m = jnp.max(t, axis=-1, keepdims=True)
    return jnp.exp2((t - m) * (scale * LOG2E)).astype(BF16)


def _with_ones(v):
    return jnp.concatenate([v, jnp.ones_like(v)], axis=-1)


def _normalised(o):
    hd = o.shape[1] // 2
    return o[:, :hd] / o[:, hd:hd + 1]


def _softmax_pv(t, v, scale):
    return _normalised(jnp.dot(_exp_scores(t, scale), _with_ones(v), preferred_element_type=F32))


def _indexer_kernel(qi_lo_ref, qi_hi_ref, ki_ref, wi_ref, o_ref, *, n_idx_heads, q0, topk, wi_scale):
    tq, s_adm = o_ref.shape[1], o_ref.shape[2]
    q_chunk = (q0 + lax.broadcasted_iota(jnp.int32, (tq, s_adm), 0)) // CHUNK
    k_chunk = lax.broadcasted_iota(jnp.int32, (tq, s_adm), 1) // CHUNK
    admissible = k_chunk <= q_chunk
    if s_adm <= topk:
        o_ref[0] = jnp.where(admissible, 0.0, NEG_BIG).astype(o_ref.dtype)
        return

    ki = ki_ref[0].astype(BF16)
    wi = wi_ref[...] * wi_scale
    score = jnp.zeros((tq, s_adm), F32)
    half = n_idx_heads // 2
    for h in range(n_idx_heads):
        qi_ref, hh = (qi_lo_ref, h) if h < half else (qi_hi_ref, h - half)
        d = lax.dot_general(qi_ref[:, hh * IDX_DIM:(hh + 1) * IDX_DIM], ki, NT_DIMS,
                            preferred_element_type=F32)
        score = score + wi[:, h:h + 1] * jnp.maximum(d, 0.0)

    bits = pltpu.bitcast(score, jnp.int32)
    key = jnp.where(bits < 0, (bits ^ jnp.int32(0x7FFFFFFF)) + 1, bits)
    key = jnp.where(admissible, key, jnp.int32(INT_MIN))

    kf = jnp.float32(topk)
    cnt = jnp.sum(jnp.where(key >= 0, 1.0, 0.0), axis=-1, keepdims=True)
    lo0 = jnp.where(cnt >= kf, jnp.int32(0), jnp.int32(INT_MIN))

    def body(i, lo):
        cand = lo + (jnp.int32(1) << (30 - i))
        c = jnp.sum(jnp.where(key >= cand, 1.0, 0.0), axis=-1, keepdims=True)
        return jnp.where(c >= kf, cand, lo)

    thr = lax.fori_loop(0, 31, body, lo0)
    sel = jnp.logical_and(admissible, key >= thr)
    o_ref[0] = jnp.where(sel, 0.0, NEG_BIG).astype(o_ref.dtype)

    n_sel = jnp.sum(jnp.where(sel, 1.0, 0.0), axis=-1, keepdims=True)

    @pl.when(jnp.max(n_sel) > kf)
    def _():
        gt = key > thr
        eq = jnp.logical_and(admissible, key == thr)
        need = kf - jnp.sum(jnp.where(gt, 1.0, 0.0), axis=-1, keepdims=True)
        idx = lax.broadcasted_iota(jnp.int32, (tq, s_adm), 1)
        nbits = max(1, (s_adm - 1).bit_length())

        def tie_body(i, last):
            cand = last + (jnp.int32(1) << (nbits - 1 - i))
            c = jnp.sum(jnp.where(jnp.logical_and(eq, idx <= cand), 1.0, 0.0), axis=-1, keepdims=True)
            return jnp.where(c < need, cand, last)

        last = lax.fori_loop(0, nbits, tie_body, jnp.full((tq, 1), -1, jnp.int32))
        keep = jnp.logical_or(gt, jnp.logical_and(eq, idx <= last + 1))
        o_ref[0] = jnp.where(keep, 0.0, NEG_BIG).astype(o_ref.dtype)


def indexer_mask(proj, kw, *, qi_col0, n_idx_heads, wi_scale, bsz, seq, qb, tq, topk):
    s_adm = (qb + 1) * tq
    nqb = seq // tq
    half_w = n_idx_heads // 2 * IDX_DIM
    assert n_idx_heads % 2 == 0 and qi_col0 % half_w == 0 and IDX_DIM == LANES
    cb = qi_col0 // half_w
    kw3 = kw.reshape(bsz, seq, kw.shape[1])
    return pl.pallas_call(
        functools.partial(_indexer_kernel, n_idx_heads=n_idx_heads, q0=qb * tq, topk=topk,
                          wi_scale=wi_scale),
        grid=(bsz,),
        in_specs=[pl.BlockSpec((tq, half_w), lambda b: (b * nqb + qb, cb)),
                  pl.BlockSpec((tq, half_w), lambda b: (b * nqb + qb, cb + 1)),
                  pl.BlockSpec((1, s_adm, IDX_DIM), lambda b: (b, 0, 0)),
                  pl.BlockSpec((tq, LANES), lambda b: (b * nqb + qb, 1))],
        out_specs=pl.BlockSpec((1, tq, s_adm), lambda b: (b, 0, 0)),
        out_shape=jax.ShapeDtypeStruct((bsz, tq, s_adm), F32),
        compiler_params=_cparams(("parallel",)),
        name="dsa_indexer",
    )(proj, proj, kw3, kw)


DSA_HEADS_PER_STEP = 4


def _dsa_attn_kernel(q_ref, k_ref, v_ref, *rest, tq):
    mask_refs, o_ref = rest[:-1], rest[-1]
    scale = HEAD_DIM ** -0.5
    for qb, m_ref in enumerate(mask_refs):
        s_adm = (qb + 1) * tq
        rows = slice(qb * tq, (qb + 1) * tq)
        k = k_ref[0, :s_adm, :]
        v1 = _with_ones(v_ref[0, :s_adm, :])
        mask = m_ref[0]
        ps = []
        for j in range(DSA_HEADS_PER_STEP):
            cols = slice(j * HEAD_DIM, (j + 1) * HEAD_DIM)
            t = lax.dot_general(q_ref[rows, cols], k, NT_DIMS, preferred_element_type=F32) + mask
            ps.append(_exp_scores(t, scale))
        o = jnp.dot(jnp.concatenate(ps, axis=0), v1, preferred_element_type=F32)
        for j in range(DSA_HEADS_PER_STEP):
            cols = slice(j * HEAD_DIM, (j + 1) * HEAD_DIM)
            o_ref[rows, cols] = _normalised(o[j * tq:(j + 1) * tq]).astype(o_ref.dtype)


def dsa_attention(proj, masks, *, n_heads, bsz, seq, tq):
    m = proj.shape[0]
    n_kv = n_heads // KV_GROUP
    proj3 = proj.reshape(bsz, seq, proj.shape[1])
    hw = DSA_HEADS_PER_STEP * HEAD_DIM
    per_group = KV_GROUP // DSA_HEADS_PER_STEP
    q_blk = lambda b, g, j: (b, g * per_group + j)
    in_specs = [pl.BlockSpec((seq, hw), q_blk),
                pl.BlockSpec((1, seq, HEAD_DIM), lambda b, g, j: (b, 0, n_heads + g)),
                pl.BlockSpec((1, seq, HEAD_DIM), lambda b, g, j: (b, 0, n_heads + n_kv + g))]
    in_specs += [pl.BlockSpec((1, tq, mk.shape[2]), lambda b, g, j: (b, 0, 0),
                              pipeline_mode=pl.Buffered(1)) for mk in masks]
    return pl.pallas_call(
        functools.partial(_dsa_attn_kernel, tq=tq),
        grid=(bsz, n_kv, per_group),
        in_specs=in_specs,
        out_specs=pl.BlockSpec((seq, hw), q_blk),
        out_shape=jax.ShapeDtypeStruct((m, n_heads * HEAD_DIM), BF16),
        compiler_params=_cparams(("parallel", "parallel", "parallel")),
        name="dsa_attn",
    )(proj, proj3, proj3, *masks)


BAND_HEADS_PER_STEP = 8


def _band_attn_kernel(q_ref, k_ref, v_ref, g_ref, o_ref, bias_ref, *, tq, pad):
    seq = q_ref.shape[1]
    hps = bias_ref.shape[0]
    width = tq + pad
    scale = HEAD_DIM ** -0.5

    @pl.when(pl.program_id(1) == 0)
    def _():
        lp = g_ref.shape[2]
        tc = lax.broadcasted_iota(jnp.int32, (tq, width), 0) // CHUNK
        jc = lax.broadcasted_iota(jnp.int32, (tq, width), 1) // CHUNK
        in_band = jnp.logical_and(jc >= tc, jc <= tc + LEFT_CHUNKS)
        for hh in range(hps):
            rows = jnp.broadcast_to(g_ref[hh], (tq, lp))
            toe = pltpu.roll(rows, lp - tq + 1, 1, stride=1, stride_axis=0)[:, :width]
            bias_ref[hh] = jnp.where(in_band, toe * (1.0 / scale), NEG_BIG)

    for qb in range(seq // tq):
        q0 = qb * tq
        klo = max(0, q0 - pad)
        nk = q0 + tq - klo
        for hh in range(hps):
            cols = slice(hh * HEAD_DIM, (hh + 1) * HEAD_DIM)
            t = lax.dot_general(q_ref[0, q0:q0 + tq, cols], k_ref[0, klo:klo + nk, cols], NT_DIMS,
                                preferred_element_type=F32) + bias_ref[hh, :, width - nk:]
            o_ref[0, q0:q0 + tq, cols] = _softmax_pv(t, v_ref[0, klo:klo + nk, cols],
                                                     scale).astype(o_ref.dtype)


def band_attention(qkv3, rel_rows, *, tq):
    bsz, seq, d3 = qkv3.shape
    n_heads = d3 // (3 * HEAD_DIM)
    pad = LEFT_CHUNKS * CHUNK
    hps = math.gcd(BAND_HEADS_PER_STEP, n_heads)
    steps = n_heads // hps
    w = hps * HEAD_DIM
    return pl.pallas_call(
        functools.partial(_band_attn_kernel, tq=tq, pad=pad),
        grid=(steps, bsz),
        in_specs=[pl.BlockSpec((1, seq, w), lambda h, b: (b, 0, h)),
                  pl.BlockSpec((1, seq, w), lambda h, b: (b, 0, steps + h)),
                  pl.BlockSpec((1, seq, w), lambda h, b: (b, 0, 2 * steps + h)),
                  pl.BlockSpec((hps, 1, rel_rows.shape[2]), lambda h, b: (h, 0, 0))],
        out_specs=pl.BlockSpec((1, seq, w), lambda h, b: (b, 0, h)),
        out_shape=jax.ShapeDtypeStruct((bsz, seq, n_heads * HEAD_DIM), BF16),
        scratch_shapes=[pltpu.VMEM((hps, tq, tq + pad), F32)],
        compiler_params=_cparams(("parallel", "arbitrary")),
        name="band_attn",
    )(qkv3, qkv3, qkv3, rel_rows)


def band_rel_rows(rel_bias, tq):
    pad = LEFT_CHUNKS * CHUNK
    lp = -(-(2 * tq + pad - 1) // LANES) * LANES
    rel = tq + pad - 1 - jnp.arange(lp)
    idx = jnp.clip(rel, -REL_CLIP, REL_CLIP) + REL_CLIP
    return rel_bias[:, idx].astype(F32)[:, None, :]


def _mem_block_kernel(x_ref, g1_ref, wq_ref, k_ref, v_ref, wo_ref, g2_ref, x_out_ref, h_out_ref):
    x = x_ref[...]
    h = _rms(x, g1_ref[...]).astype(BF16)
    q = jnp.dot(h, wq_ref[...], preferred_element_type=F32).astype(BF16)
    scale = MEM_HEAD_DIM ** -0.5
    outs = []
    for hd in range(MEM_HEADS):
        sl = slice(hd * MEM_HEAD_DIM, (hd + 1) * MEM_HEAD_DIM)
        t = lax.dot_general(q[:, sl], k_ref[0, :, sl], NT_DIMS, preferred_element_type=F32)
        outs.append(_softmax_pv(t, v_ref[0, :, sl], scale).astype(BF16))
    o = jnp.concatenate(outs, axis=-1)
    y = x + jnp.dot(o, wo_ref[...], preferred_element_type=F32)
    x_out_ref[...] = y
    h_out_ref[...] = _rms(y, g2_ref[...]).astype(h_out_ref.dtype)


def mem_block(x, g1, wq, km3, vm3, wo, g2, *, seq, tm=512):
    m, d = x.shape
    md = wq.shape[1]
    n_mem = km3.shape[1]
    per_b = seq // tm
    const2 = lambda i: (0, 0)
    once = pl.Buffered(1)
    return pl.pallas_call(
        _mem_block_kernel,
        grid=(m // tm,),
        in_specs=[pl.BlockSpec((tm, d), lambda i: (i, 0)),
                  pl.BlockSpec((1, d), const2, pipeline_mode=once),
                  pl.BlockSpec((d, md), const2, pipeline_mode=once),
                  pl.BlockSpec((1, n_mem, md), lambda i: (i // per_b, 0, 0)),
                  pl.BlockSpec((1, n_mem, md), lambda i: (i // per_b, 0, 0)),
                  pl.BlockSpec((md, d), const2, pipeline_mode=once),
                  pl.BlockSpec((1, d), const2, pipeline_mode=once)],
        out_specs=[pl.BlockSpec((tm, d), lambda i: (i, 0)),
                   pl.BlockSpec((tm, d), lambda i: (i, 0))],
        out_shape=[jax.ShapeDtypeStruct((m, d), F32),
                   jax.ShapeDtypeStruct((m, d), BF16)],
        compiler_params=_cparams(("parallel",)),
        name="mem_block",
    )(x, g1.reshape(1, d), wq, km3, vm3, wo, g2.reshape(1, d))


def _rope_tables(seq):
    half = HEAD_DIM // 2
    inv = ROPE_THETA ** (-jnp.arange(half, dtype=F32) / half)
    ang = jnp.arange(seq, dtype=F32)[:, None] * inv[None, :]
    cos, sin = jnp.cos(ang), jnp.sin(ang)
    return jnp.concatenate([cos, cos], axis=-1), jnp.concatenate([-sin, sin], axis=-1)


def kernel(x, mem, g_mix, g_mem_attn, g_ffn, g_memory, g_final, w_in_a, w_out_a, w_in_b, rel_bias,
           w_out_b, w_mq, w_mk, w_mv, w_mo, w_up, w_down):
    bsz, seq, d = x.shape
    depth = g_mix.shape[0]
    m = bsz * seq
    n_heads = d // HEAD_DIM
    n_kv = n_heads // KV_GROUP
    a_q, a_kv, a_iq = n_heads * HEAD_DIM, n_kv * HEAD_DIM, n_heads * IDX_DIM
    n_idx_heads = n_heads
    topk = min(TOPK_MAX, seq // 4)
    tq = ATTN_Q_BLOCK
    bm, bn = MM_BLOCK_M, MM_BLOCK_N

    xf = x.reshape(m, d)
    cos, sin = _rope_tables(seq)

    n_mem = mem.shape[1]
    mem_n = rmsnorm(mem.reshape(bsz * n_mem, d), g_memory, BF16)

    for i in range(depth):
        j = i // 2
        h = rmsnorm(xf, g_mix[i], BF16)
        if i % 2 == 0:
            c1, c2, c3 = a_q + a_kv, a_q + 2 * a_kv, a_q + 2 * a_kv + a_iq
            wt = jnp.swapaxes(w_in_a, 1, 2)
            proj = matmul(h, wt, w_t=True, layer=j, n=c3, bm=bm, bn=bn, out_dtype=BF16, kind="rope",
                          rope=(cos, sin, seq, (c1, c2)))
            w_kw = jnp.pad(wt[j:j + 1, c3:], ((0, 0), (0, 2 * LANES - (wt.shape[1] - c3)), (0, 0)))
            kw = matmul(h, w_kw, w_t=True, bm=bm, bn=2 * LANES, out_dtype=F32, kind="rope",
                        rope=(cos, sin, seq, (IDX_DIM, 2 * LANES)))
            wi_scale = (n_idx_heads ** -0.5) * (IDX_DIM ** -0.5)
            masks = [indexer_mask(proj, kw, qi_col0=c2, n_idx_heads=n_idx_heads, wi_scale=wi_scale, bsz=bsz,
                                  seq=seq, qb=qb, tq=tq, topk=topk) for qb in range(seq // tq)]
            o = dsa_attention(proj, masks, n_heads=n_heads, bsz=bsz, seq=seq, tq=tq)
            xf = matmul(o, w_out_a, layer=j, bm=bm, bn=bn, out_dtype=F32, kind="residual", res=xf)
        else:
            qkv = matmul(h, w_in_b, layer=j, bm=bm, bn=bn, out_dtype=BF16)
            o = band_attention(qkv.reshape(bsz, seq, 3 * d), band_rel_rows(rel_bias[j], tq), tq=tq)
            xf = matmul(o.reshape(m, d), w_out_b, layer=j, bm=bm, bn=bn, out_dtype=F32,
                        kind="residual", res=xf)

        md = w_mq.shape[2]
        km = matmul(mem_n, w_mk, layer=i, bm=bm, bn=md, out_dtype=BF16).reshape(bsz, n_mem, md)
        vm = matmul(mem_n, w_mv, layer=i, bm=bm, bn=md, out_dtype=BF16).reshape(bsz, n_mem, md)
        xf, h2 = mem_block(xf, g_mem_attn[i], w_mq[i].astype(BF16), km, vm, w_mo[i].astype(BF16),
                           g_ffn[i], seq=seq)

        u = matmul(h2, w_up, layer=i, bm=bm, bn=bn, out_dtype=BF16, kind="relu2")
        xf = matmul(u, w_down, layer=i, bm=bm, bn=bn, bk=MM_BLOCK_K, bn_first=bn, bk_first=MM_BLOCK_K // 2,
                    out_dtype=F32, kind="residual", res=xf)

    return rmsnorm(xf, g_final, F32).reshape(bsz, seq, d)
```

```python
import functools
import math

import jax
import jax.numpy as jnp
from jax import lax
from jax.experimental import pallas as pl
from jax.experimental.pallas import tpu as pltpu

CHUNK = 64
HEAD_DIM = 128
KV_GROUP = 4
IDX_DIM = 128
TOPK_MAX = 256
LEFT_CHUNKS = 8
REL_CLIP = 256
MEM_HEADS = 4
MEM_HEAD_DIM = 128
ROPE_THETA = 10000.0
EPS = 1e-6

LANES = 128
V7X_VMEM_BYTES = 64 * 1024 * 1024
VMEM_LIMIT = V7X_VMEM_BYTES - 8 * 1024 * 1024

MM_BLOCK_M = 1024
MM_BLOCK_N = 1024
MM_BLOCK_K = 4096
MM_FIRST_BLOCK_N = 512
RES_DOT_CHUNK = 512
ATTN_Q_BLOCK = 256

LOG2E = 1.4426950408889634
NEG_BIG = -1e30
INT_MIN = -(2 ** 31)

F32 = jnp.float32
BF16 = jnp.bfloat16
NT_DIMS = (((1,), (1,)), ((), ()))


def _cparams(sem):
    return pltpu.CompilerParams(dimension_semantics=sem, vmem_limit_bytes=VMEM_LIMIT)


def _rms(x, g):
    return x * lax.rsqrt(jnp.mean(x * x, axis=-1, keepdims=True) + EPS) * g


def _rmsnorm_kernel(x_ref, g_ref, o_ref):
    o_ref[...] = _rms(x_ref[...], g_ref[...]).astype(o_ref.dtype)


def rmsnorm(x, g, out_dtype, tm=512):
    m, d = x.shape
    tm = min(tm, m)
    return pl.pallas_call(
        _rmsnorm_kernel,
        grid=(m // tm,),
        in_specs=[pl.BlockSpec((tm, d), lambda i: (i, 0)),
                  pl.BlockSpec((1, d), lambda i: (0, 0))],
        out_specs=pl.BlockSpec((tm, d), lambda i: (i, 0)),
        out_shape=jax.ShapeDtypeStruct((m, d), out_dtype),
        compiler_params=_cparams(("parallel",)),
        name="rmsnorm",
    )(x, g.reshape(1, d))


def _epilogue(acc, kind, extra_refs, o_ref, plain_cols):
    if kind == "cast":
        o_ref[...] = acc.astype(o_ref.dtype)
    elif kind == "relu2":
        a = jnp.maximum(acc, 0.0)
        o_ref[...] = (a * a).astype(o_ref.dtype)
    elif kind == "rope":
        cos = extra_refs[0][...]
        sin = extra_refs[1][...]
        bn = acc.shape[1]
        for c in range(bn // HEAD_DIM):
            xs = acc[:, c * HEAD_DIM:(c + 1) * HEAD_DIM]
            y = xs * cos + pltpu.roll(xs, HEAD_DIM // 2, 1) * sin
            if plain_cols is not None:
                col = pl.program_id(1) * bn + c * HEAD_DIM
                y = jnp.where(jnp.logical_and(col >= plain_cols[0], col < plain_cols[1]), xs, y)
            o_ref[:, c * HEAD_DIM:(c + 1) * HEAD_DIM] = y.astype(o_ref.dtype)
    else:
        raise ValueError(kind)


def _mm_kernel(a_ref, b_ref, *rest, kind, nk, n_extra, emit_w, w_t, has_prev, plain_cols):
    extra = rest[:n_extra]
    outs = rest[n_extra + (1 if has_prev else 0):]
    o_ref = outs[0]
    w_ref = b_ref
    if emit_w:
        w_ref = outs[1]
        w32 = b_ref[...].T if w_t else b_ref[...]
        w_ref[...] = w32.astype(BF16)
    if kind != "residual":
        acc = jnp.dot(a_ref[...], w_ref[...], preferred_element_type=F32)
        _epilogue(acc, kind, extra, o_ref, plain_cols)
        return
    if nk > 1:
        @pl.when(pl.program_id(2) == 0)
        def _():
            o_ref[...] = extra[0][...]

    bn = o_ref.shape[1]
    cw = min(bn, RES_DOT_CHUNK)
    for c in range(bn // cw):
        cols = slice(c * cw, (c + 1) * cw)
        part = jnp.dot(a_ref[...], w_ref[:, cols], preferred_element_type=F32)
        if nk == 1:
            o_ref[:, cols] = extra[0][:, cols] + part
        else:
            o_ref[:, cols] += part


def _mm_call(a, b, y_prev, *, row0, nrows, col0, n, bm, bn, bk, out_dtype, kind, res, rope,
             emit_w, layer=0, w_t=False):
    m, kdim = a.shape
    nk = kdim // bk
    jb0 = col0 // bn
    assert n % bn == 0 and kdim % bk == 0 and col0 % bn == 0
    assert nk == 1 or (kind == "residual" and out_dtype == F32)
    if w_t:
        assert emit_w and b.ndim == 3
        b_spec = pl.BlockSpec((None, bn, bk), lambda i, j, k: (layer, j + jb0, k))
    elif b.ndim == 3:
        b_spec = pl.BlockSpec((None, bk, bn), lambda i, j, k: (layer, k, j + jb0))
    else:
        b_spec = pl.BlockSpec((bk, bn), lambda i, j, k: (k, j + jb0))
    in_specs = [pl.BlockSpec((bm, bk), lambda i, j, k: (i + row0, k)), b_spec]
    args = [a, b]
    if kind == "residual":
        in_specs.append(pl.BlockSpec((bm, bn), lambda i, j, k: (i + row0, j)))
        args.append(res)
    plain_cols = None
    if kind == "rope":
        cos, sin, seq, plain_cols = rope
        nb = seq // bm
        assert seq % bm == 0
        assert plain_cols is None or all(c % HEAD_DIM == 0 for c in plain_cols)
        for t in (cos, sin):
            in_specs.append(pl.BlockSpec((bm, HEAD_DIM), lambda i, j, k: ((i + row0) % nb, 0)))
            args.append(t)
    n_extra = len(args) - 2
    aliases = {}
    if y_prev is not None:
        aliases = {len(args): 0}
        in_specs.append(pl.BlockSpec(memory_space=pl.ANY))
        args.append(y_prev)
    out_specs = [pl.BlockSpec((bm, bn), lambda i, j, k: (i + row0, j))]
    out_shape = [jax.ShapeDtypeStruct((m, n), out_dtype)]
    if emit_w:
        out_specs.append(pl.BlockSpec((bk, bn), lambda i, j, k: (k, j)))
        out_shape.append(jax.ShapeDtypeStruct((kdim, n), BF16))
    return pl.pallas_call(
        functools.partial(_mm_kernel, kind=kind, nk=nk, n_extra=n_extra, emit_w=emit_w,
                          w_t=w_t, has_prev=y_prev is not None, plain_cols=plain_cols),
        grid=(nrows, n // bn, nk),
        in_specs=in_specs,
        out_specs=out_specs,
        out_shape=out_shape,
        input_output_aliases=aliases,
        compiler_params=_cparams(("parallel", "parallel", "arbitrary")),
        name="mm_" + kind + ("_w" if emit_w else ""),
    )(*args)


def matmul(a, w32, *, bm, bn, bk=None, out_dtype, kind="cast", res=None, rope=None, col0=0, n=None,
           layer=0, bn_first=MM_FIRST_BLOCK_N, bk_first=None, w_t=False):
    m, kdim = a.shape
    n = w32.shape[-2 if w_t else -1] if n is None else n
    bk = kdim if bk is None else min(bk, kdim)
    bk_first = bk if bk_first is None else min(bk_first, kdim)
    bm = min(bm, m)
    assert m % bm == 0 and w32.shape[-1 if w_t else -2] == kdim
    common = dict(n=n, bm=bm, out_dtype=out_dtype, kind=kind, res=res, rope=rope)
    y, wb = _mm_call(a, w32, None, row0=0, nrows=1, col0=col0, bn=math.gcd(bn_first, n, col0),
                     bk=bk_first, emit_w=True, layer=layer, w_t=w_t, **common)
    if m > bm:
        y = _mm_call(a, wb, y, row0=1, nrows=m // bm - 1, col0=0, bn=math.gcd(bn, n), bk=bk,
                     emit_w=False, **common)[0]
    return y


def _exp_scores(t, scale):
    m = jnp.max(t, axis=-1, keepdims=True)
    return jnp.exp2((t - m) * (scale * LOG2E)).astype(BF16)


def _with_ones(v):
    return jnp.concatenate([v, jnp.ones_like(v)], axis=-1)


def _normalised(o):
    hd = o.shape[1] // 2
    return o[:, :hd] / o[:, hd:hd + 1]


def _softmax_pv(t, v, scale):
    return _normalised(jnp.dot(_exp_scores(t, scale), _with_ones(v), preferred_element_type=F32))


def _indexer_kernel(qi_lo_ref, qi_hi_ref, ki_ref, wi_ref, o_ref, *, n_idx_heads, q0, topk, wi_scale):
    tq, s_adm = o_ref.shape[1], o_ref.shape[2]
    q_chunk = (q0 + lax.broadcasted_iota(jnp.int32, (tq, s_adm), 0)) // CHUNK
    k_chunk = lax.broadcasted_iota(jnp.int32, (tq, s_adm), 1) // CHUNK
    admissible = k_chunk <= q_chunk
    if s_adm <= topk:
        o_ref[0] = jnp.where(admissible, 0.0, NEG_BIG).astype(o_ref.dtype)
        return

    ki = ki_ref[0].astype(BF16)
    wi = wi_ref[...] * wi_scale
    score = jnp.zeros((tq, s_adm), F32)
    half = n_idx_heads // 2
    for h in range(n_idx_heads):
        qi_ref, hh = (qi_lo_ref, h) if h < half else (qi_hi_ref, h - half)
        d = lax.dot_general(qi_ref[:, hh * IDX_DIM:(hh + 1) * IDX_DIM], ki, NT_DIMS,
                            preferred_element_type=F32)
        score = score + wi[:, h:h + 1] * jnp.maximum(d, 0.0)

    bits = pltpu.bitcast(score, jnp.int32)
    key = jnp.where(bits < 0, (bits ^ jnp.int32(0x7FFFFFFF)) + 1, bits)
    key = jnp.where(admissible, key, jnp.int32(INT_MIN))

    kf = jnp.float32(topk)
    cnt = jnp.sum(jnp.where(key >= 0, 1.0, 0.0), axis=-1, keepdims=True)
    lo0 = jnp.where(cnt >= kf, jnp.int32(0), jnp.int32(INT_MIN))

    def body(i, lo):
        cand = lo + (jnp.int32(1) << (30 - i))
        c = jnp.sum(jnp.where(key >= cand, 1.0, 0.0), axis=-1, keepdims=True)
        return jnp.where(c >= kf, cand, lo)

    thr = lax.fori_loop(0, 31, body, lo0)
    sel = jnp.logical_and(admissible, key >= thr)
    o_ref[0] = jnp.where(sel, 0.0, NEG_BIG).astype(o_ref.dtype)

    n_sel = jnp.sum(jnp.where(sel, 1.0, 0.0), axis=-1, keepdims=True)

    @pl.when(jnp.max(n_sel) > kf)
    def _():
        gt = key > thr
        eq = jnp.logical_and(admissible, key == thr)
        need = kf - jnp.sum(jnp.where(gt, 1.0, 0.0), axis=-1, keepdims=True)
        idx = lax.broadcasted_iota(jnp.int32, (tq, s_adm), 1)
        nbits = max(1, (s_adm - 1).bit_length())

        def tie_body(i, last):
            cand = last + (jnp.int32(1) << (nbits - 1 - i))
            c = jnp.sum(jnp.where(jnp.logical_and(eq, idx <= cand), 1.0, 0.0), axis=-1, keepdims=True)
            return jnp.where(c < need, cand, last)

        last = lax.fori_loop(0, nbits, tie_body, jnp.full((tq, 1), -1, jnp.int32))
        keep = jnp.logical_or(gt, jnp.logical_and(eq, idx <= last + 1))
        o_ref[0] = jnp.where(keep, 0.0, NEG_BIG).astype(o_ref.dtype)


def indexer_mask(proj, kw, *, qi_col0, n_idx_heads, wi_scale, bsz, seq, qb, tq, topk):
    s_adm = (qb + 1) * tq
    nqb = seq // tq
    half_w = n_idx_heads // 2 * IDX_DIM
    assert n_idx_heads % 2 == 0 and qi_col0 % half_w == 0 and IDX_DIM == LANES
    cb = qi_col0 // half_w
    kw3 = kw.reshape(bsz, seq, kw.shape[1])
    return pl.pallas_call(
        functools.partial(_indexer_kernel, n_idx_heads=n_idx_heads, q0=qb * tq, topk=topk,
                          wi_scale=wi_scale),
        grid=(bsz,),
        in_specs=[pl.BlockSpec((tq, half_w), lambda b: (b * nqb + qb, cb)),
                  pl.BlockSpec((tq, half_w), lambda b: (b * nqb + qb, cb + 1)),
                  pl.BlockSpec((1, s_adm, IDX_DIM), lambda b: (b, 0, 0)),
                  pl.BlockSpec((tq, LANES), lambda b: (b * nqb + qb, 1))],
        out_specs=pl.BlockSpec((1, tq, s_adm), lambda b: (b, 0, 0)),
        out_shape=jax.ShapeDtypeStruct((bsz, tq, s_adm), F32),
        compiler_params=_cparams(("parallel",)),
        name="dsa_indexer",
    )(proj, proj, kw3, kw)


DSA_KV_HEADS_PER_STEP = 2


def _dsa_attn_kernel(q_ref, k_ref, v_ref, *rest, tq):
    mask_refs, o_ref = rest[:-1], rest[-1]
    scale = HEAD_DIM ** -0.5
    for qb, m_ref in enumerate(mask_refs):
        s_adm = (qb + 1) * tq
        rows = slice(qb * tq, (qb + 1) * tq)
        mask = m_ref[0]
        for g in range(k_ref.shape[2] // HEAD_DIM):
            kv_cols = slice(g * HEAD_DIM, (g + 1) * HEAD_DIM)
            k = k_ref[0, :s_adm, kv_cols]
            v1 = _with_ones(v_ref[0, :s_adm, kv_cols])
            heads = [slice((g * KV_GROUP + j) * HEAD_DIM, (g * KV_GROUP + j + 1) * HEAD_DIM)
                     for j in range(KV_GROUP)]
            ps = [_exp_scores(lax.dot_general(q_ref[rows, cols], k, NT_DIMS,
                                              preferred_element_type=F32) + mask, scale)
                  for cols in heads]
            o = jnp.dot(jnp.concatenate(ps, axis=0), v1, preferred_element_type=F32)
            for j, cols in enumerate(heads):
                o_ref[rows, cols] = _normalised(o[j * tq:(j + 1) * tq]).astype(o_ref.dtype)


def dsa_attention(proj, masks, *, n_heads, bsz, seq, tq):
    m = proj.shape[0]
    n_kv = n_heads // KV_GROUP
    proj3 = proj.reshape(bsz, seq, proj.shape[1])
    gps = math.gcd(DSA_KV_HEADS_PER_STEP, n_kv, n_heads)
    qw, kvw = gps * KV_GROUP * HEAD_DIM, gps * HEAD_DIM
    k0, v0 = n_heads // gps, (n_heads + n_kv) // gps
    in_specs = [pl.BlockSpec((seq, qw), lambda b, g: (b, g)),
                pl.BlockSpec((1, seq, kvw), lambda b, g: (b, 0, k0 + g)),
                pl.BlockSpec((1, seq, kvw), lambda b, g: (b, 0, v0 + g))]
    in_specs += [pl.BlockSpec((1, tq, mk.shape[2]), lambda b, g: (b, 0, 0),
                              pipeline_mode=pl.Buffered(1)) for mk in masks]
    return pl.pallas_call(
        functools.partial(_dsa_attn_kernel, tq=tq),
        grid=(bsz, n_kv // gps),
        in_specs=in_specs,
        out_specs=pl.BlockSpec((seq, qw), lambda b, g: (b, g)),
        out_shape=jax.ShapeDtypeStruct((m, n_heads * HEAD_DIM), BF16),
        compiler_params=_cparams(("parallel", "parallel")),
        name="dsa_attn",
    )(proj, proj3, proj3, *masks)


BAND_HEADS_PER_STEP = 8


def _band_attn_kernel(q_ref, k_ref, v_ref, g_ref, o_ref, bias_ref, *, tq, pad):
    seq = q_ref.shape[1]
    hps = bias_ref.shape[0]
    width = tq + pad
    scale = HEAD_DIM ** -0.5

    @pl.when(pl.program_id(1) == 0)
    def _():
        lp = g_ref.shape[2]
        tc = lax.broadcasted_iota(jnp.int32, (tq, width), 0) // CHUNK
        jc = lax.broadcasted_iota(jnp.int32, (tq, width), 1) // CHUNK
        in_band = jnp.logical_and(jc >= tc, jc <= tc + LEFT_CHUNKS)
        for hh in range(hps):
            rows = jnp.broadcast_to(g_ref[hh], (tq, lp))
            toe = pltpu.roll(rows, lp - tq + 1, 1, stride=1, stride_axis=0)[:, :width]
            bias_ref[hh] = jnp.where(in_band, toe * (1.0 / scale), NEG_BIG)

    for qb in range(seq // tq):
        q0 = qb * tq
        klo = max(0, q0 - pad)
        nk = q0 + tq - klo
        for hh in range(hps):
            cols = slice(hh * HEAD_DIM, (hh + 1) * HEAD_DIM)
            t = lax.dot_general(q_ref[0, q0:q0 + tq, cols], k_ref[0, klo:klo + nk, cols], NT_DIMS,
                                preferred_element_type=F32) + bias_ref[hh, :, width - nk:]
            o_ref[0, q0:q0 + tq, cols] = _softmax_pv(t, v_ref[0, klo:klo + nk, cols],
                                                     scale).astype(o_ref.dtype)


def band_attention(qkv3, rel_rows, *, tq):
    bsz, seq, d3 = qkv3.shape
    n_heads = d3 // (3 * HEAD_DIM)
    pad = LEFT_CHUNKS * CHUNK
    hps = math.gcd(BAND_HEADS_PER_STEP, n_heads)
    steps = n_heads // hps
    w = hps * HEAD_DIM
    return pl.pallas_call(
        functools.partial(_band_attn_kernel, tq=tq, pad=pad),
        grid=(steps, bsz),
        in_specs=[pl.BlockSpec((1, seq, w), lambda h, b: (b, 0, h)),
                  pl.BlockSpec((1, seq, w), lambda h, b: (b, 0, steps + h)),
                  pl.BlockSpec((1, seq, w), lambda h, b: (b, 0, 2 * steps + h)),
                  pl.BlockSpec((hps, 1, rel_rows.shape[2]), lambda h, b: (h, 0, 0))],
        out_specs=pl.BlockSpec((1, seq, w), lambda h, b: (b, 0, h)),
        out_shape=jax.ShapeDtypeStruct((bsz, seq, n_heads * HEAD_DIM), BF16),
        scratch_shapes=[pltpu.VMEM((hps, tq, tq + pad), F32)],
        compiler_params=_cparams(("parallel", "arbitrary")),
        name="band_attn",
    )(qkv3, qkv3, qkv3, rel_rows)


def band_rel_rows(rel_bias, tq):
    pad = LEFT_CHUNKS * CHUNK
    lp = -(-(2 * tq + pad - 1) // LANES) * LANES
    rel = tq + pad - 1 - jnp.arange(lp)
    idx = jnp.clip(rel, -REL_CLIP, REL_CLIP) + REL_CLIP
    return rel_bias[:, idx].astype(F32)[:, None, :]


def _mem_block_kernel(x_ref, g1_ref, wq_ref, k_ref, v_ref, wo_ref, g2_ref, x_out_ref, h_out_ref):
    x = x_ref[...]
    h = _rms(x, g1_ref[...]).astype(BF16)
    q = jnp.dot(h, wq_ref[...], preferred_element_type=F32).astype(BF16)
    scale = MEM_HEAD_DIM ** -0.5
    outs = []
    for hd in range(MEM_HEADS):
        sl = slice(hd * MEM_HEAD_DIM, (hd + 1) * MEM_HEAD_DIM)
        t = lax.dot_general(q[:, sl], k_ref[0, :, sl], NT_DIMS, preferred_element_type=F32)
        outs.append(_softmax_pv(t, v_ref[0, :, sl], scale).astype(BF16))
    o = jnp.concatenate(outs, axis=-1)
    y = x + jnp.dot(o, wo_ref[...], preferred_element_type=F32)
    x_out_ref[...] = y
    h_out_ref[...] = _rms(y, g2_ref[...]).astype(h_out_ref.dtype)


def mem_block(x, g1, wq, km3, vm3, wo, g2, *, seq, tm=512):
    m, d = x.shape
    md = wq.shape[1]
    n_mem = km3.shape[1]
    per_b = seq // tm
    const2 = lambda i: (0, 0)
    once = pl.Buffered(1)
    return pl.pallas_call(
        _mem_block_kernel,
        grid=(m // tm,),
        in_specs=[pl.BlockSpec((tm, d), lambda i: (i, 0)),
                  pl.BlockSpec((1, d), const2, pipeline_mode=once),
                  pl.BlockSpec((d, md), const2, pipeline_mode=once),
                  pl.BlockSpec((1, n_mem, md), lambda i: (i // per_b, 0, 0)),
                  pl.BlockSpec((1, n_mem, md), lambda i: (i // per_b, 0, 0)),
                  pl.BlockSpec((md, d), const2, pipeline_mode=once),
                  pl.BlockSpec((1, d), const2, pipeline_mode=once)],
        out_specs=[pl.BlockSpec((tm, d), lambda i: (i, 0)),
                   pl.BlockSpec((tm, d), lambda i: (i, 0))],
        out_shape=[jax.ShapeDtypeStruct((m, d), F32),
                   jax.ShapeDtypeStruct((m, d), BF16)],
        compiler_params=_cparams(("parallel",)),
        name="mem_block",
    )(x, g1.reshape(1, d), wq, km3, vm3, wo, g2.reshape(1, d))


def _rope_tables(seq):
    half = HEAD_DIM // 2
    inv = ROPE_THETA ** (-jnp.arange(half, dtype=F32) / half)
    ang = jnp.arange(seq, dtype=F32)[:, None] * inv[None, :]
    cos, sin = jnp.cos(ang), jnp.sin(ang)
    return jnp.concatenate([cos, cos], axis=-1), jnp.concatenate([-sin, sin], axis=-1)


def kernel(x, mem, g_mix, g_mem_attn, g_ffn, g_memory, g_final, w_in_a, w_out_a, w_in_b, rel_bias,
           w_out_b, w_mq, w_mk, w_mv, w_mo, w_up, w_down):
    bsz, seq, d = x.shape
    depth = g_mix.shape[0]
    m = bsz * seq
    n_heads = d // HEAD_DIM
    n_kv = n_heads // KV_GROUP
    a_q, a_kv, a_iq = n_heads * HEAD_DIM, n_kv * HEAD_DIM, n_heads * IDX_DIM
    n_idx_heads = n_heads
    topk = min(TOPK_MAX, seq // 4)
    tq = ATTN_Q_BLOCK
    bm, bn = MM_BLOCK_M, MM_BLOCK_N

    xf = x.reshape(m, d)
    cos, sin = _rope_tables(seq)

    n_mem = mem.shape[1]
    mem_n = rmsnorm(mem.reshape(bsz * n_mem, d), g_memory, BF16)

    for i in range(depth):
        j = i // 2
        h = rmsnorm(xf, g_mix[i], BF16)
        if i % 2 == 0:
            c1, c2, c3 = a_q + a_kv, a_q + 2 * a_kv, a_q + 2 * a_kv + a_iq
            wt = jnp.swapaxes(w_in_a, 1, 2)
            proj = matmul(h, wt, w_t=True, layer=j, n=c3, bm=bm, bn=bn, out_dtype=BF16, kind="rope",
                          rope=(cos, sin, seq, (c1, c2)))
            w_kw = jnp.pad(wt[j:j + 1, c3:], ((0, 0), (0, 2 * LANES - (wt.shape[1] - c3)), (0, 0)))
            kw = matmul(h, w_kw, w_t=True, bm=bm, bn=2 * LANES, out_dtype=F32, kind="rope",
                        rope=(cos, sin, seq, (IDX_DIM, 2 * LANES)))
            wi_scale = (n_idx_heads ** -0.5) * (IDX_DIM ** -0.5)
            masks = [indexer_mask(proj, kw, qi_col0=c2, n_idx_heads=n_idx_heads, wi_scale=wi_scale, bsz=bsz,
                                  seq=seq, qb=qb, tq=tq, topk=topk) for qb in range(seq // tq)]
            o = dsa_attention(proj, masks, n_heads=n_heads, bsz=bsz, seq=seq, tq=tq)
            xf = matmul(o, w_out_a, layer=j, bm=bm, bn=bn, out_dtype=F32, kind="residual", res=xf)
        else:
            qkv = matmul(h, w_in_b, layer=j, bm=bm, bn=bn, out_dtype=BF16)
            o = band_attention(qkv.reshape(bsz, seq, 3 * d), band_rel_rows(rel_bias[j], tq), tq=tq)
            xf = matmul(o.reshape(m, d), w_out_b, layer=j, bm=bm, bn=bn, out_dtype=F32,
                        kind="residual", res=xf)

        md = w_mq.shape[2]
        km = matmul(mem_n, w_mk, layer=i, bm=bm, bn=md, out_dtype=BF16).reshape(bsz, n_mem, md)
        vm = matmul(mem_n, w_mv, layer=i, bm=bm, bn=md, out_dtype=BF16).reshape(bsz, n_mem, md)
        xf, h2 = mem_block(xf, g_mem_attn[i], w_mq[i].astype(BF16), km, vm, w_mo[i].astype(BF16),
                           g_ffn[i], seq=seq)

        u = matmul(h2, w_up, layer=i, bm=bm, bn=bn, out_dtype=BF16, kind="relu2")
        xf = matmul(u, w_down, layer=i, bm=bm, bn=bn, bk=MM_BLOCK_K, bn_first=bn, bk_first=MM_BLOCK_K // 2,
                    out_dtype=F32, kind="residual", res=xf)

    return rmsnorm(xf, g_final, F32).reshape(bsz, seq, d)
```
